```python
import math
import jax, jax.numpy as jnp
from jax import lax
import numpy as np

D_MODEL = 1024
BATCH = 2
SEQ = 16384
DEPTH = 4

SSM_WIDTH = D_MODEL // 4
SSM_GROUP_CH = 16
SSM_GROUPS = SSM_WIDTH // SSM_GROUP_CH
SSM_STATE = 64
RET_HEAD_DIM = 64
RET_WIDTH = D_MODEL // 2
RET_HEADS = RET_WIDTH // RET_HEAD_DIM
SB_HEAD_DIM = 64
SB_WIDTH = D_MODEL // 4
SB_HEADS = SB_WIDTH // SB_HEAD_DIM
MIX_WIDTH = SSM_WIDTH + RET_WIDTH + SB_WIDTH
IN_COLS = SSM_WIDTH + 4 * RET_WIDTH + 3 * SB_WIDTH
RET_CHUNK = 128
SB_BLOCK = 128
ROPE_BASE = 10000.0
D_FF = 64 * ((8 * D_MODEL // 3 + 63) // 64)
D_FF_EXPERT = D_FF // 4
N_EXPERTS = 8
TOP_K = 2
N_DENSE = (DEPTH + 1) // 2
N_MOE = DEPTH // 2
ALPHA = (2.0 * DEPTH) ** 0.25
BETA_INIT = (8.0 * DEPTH) ** -0.25
LN_EPS = 1e-5

kernel_name = "hybrid_s5_retention_stickbreaking_moe"

F32 = jnp.float32
HI = lax.Precision.HIGHEST


def layer_norm(x, w, b):
    xf = x.astype(F32)
    mu = jnp.mean(xf, -1, keepdims=True)
    var = jnp.mean(jnp.square(xf - mu), -1, keepdims=True)
    y = (xf - mu) * lax.rsqrt(var + LN_EPS) * w.astype(F32) + b.astype(F32)
    return y.astype(x.dtype)


def head_rms_norm(x, w):
    xf = x.astype(F32)
    y = xf * lax.rsqrt(jnp.mean(xf * xf, -1, keepdims=True) + LN_EPS)
    return y * w.astype(F32)


def s5_mixer(u, lam_re, lam_im, log_dt, b_re, b_im, c_re, c_im, d_skip, glu_w, glu_b, norm_w):
    bsz, seq, _ = u.shape
    G, H, P = SSM_GROUPS, SSM_GROUP_CH, SSM_STATE
    ug = u.astype(F32).reshape(bsz, seq, G, H)
    lam = lax.complex(lam_re.astype(F32), lam_im.astype(F32))
    dt = jnp.exp(log_dt.astype(F32))[:, None]
    lam_bar = jnp.exp(lam * dt)
    b = lax.complex(b_re.astype(F32), b_im.astype(F32))
    b_bar = ((lam_bar - 1.0) / lam)[..., None] * b
    c = lax.complex(c_re.astype(F32), c_im.astype(F32))
    bu = jnp.einsum('gph,bsgh->bsgp', b_bar, ug.astype(jnp.complex64))
    a = jnp.broadcast_to(lam_bar, bu.shape)

    def combine(e1, e2):
        a1, s1 = e1
        a2, s2 = e2
        return a1 * a2, a2 * s1 + s2

    _, states = lax.associative_scan(combine, (a, bu), axis=1)
    y = jnp.einsum('ghp,bsgp->bsgh', c, states).real + d_skip.astype(F32).reshape(G, H) * ug
    g = jax.nn.gelu(y.reshape(bsz, seq, SSM_WIDTH))
    out = g * jax.nn.sigmoid(g @ glu_w.astype(F32) + glu_b.astype(F32))
    out = head_rms_norm(out.reshape(bsz, seq, G, H), norm_w.reshape(G, H))
    return out.reshape(bsz, seq, SSM_WIDTH).astype(u.dtype)


def rope(x, pos):
    half = x.shape[-1] // 2
    inv = ROPE_BASE ** (-jnp.arange(half, dtype=F32) / half)
    ang = pos[:, None] * inv[None, :]
    cos = jnp.cos(ang)[:, None, :]
    sin = jnp.sin(ang)[:, None, :]
    x1, x2 = x[..., :half], x[..., half:]
    return jnp.concatenate([x1 * cos - x2 * sin, x1 * sin + x2 * cos], axis=-1)


def retention_mixer(q, k, v, g, gn_w, gn_b):
    bsz, seq, _ = q.shape
    H, dh, C = RET_HEADS, RET_HEAD_DIM, RET_CHUNK
    n = seq // C
    pos = jnp.arange(seq, dtype=F32)
    qh = rope(q.astype(F32).reshape(bsz, seq, H, dh), pos)
    kh = rope(k.astype(F32).reshape(bsz, seq, H, dh), pos) * (dh ** -0.5)
    vh = v.astype(F32).reshape(bsz, seq, H, dh)
    log_gamma = jnp.log(1.0 - 2.0 ** (-5.0 - jnp.arange(H, dtype=F32)))

    def chunk(t):
        return t.reshape(bsz, n, C, H, dh).transpose(1, 0, 3, 2, 4)

    qc, kc, vc = chunk(qh), chunk(kh), chunk(vh)
    idx = jnp.arange(C, dtype=F32)
    rel = idx[:, None] - idx[None, :]
    decay_mask = jnp.where(rel >= 0, jnp.exp(log_gamma[:, None, None] * jnp.maximum(rel, 0.0)), 0.0)
    scores = jnp.einsum('nbhqd,nbhkd->nbhqk', qc, kc) * decay_mask
    inner = jnp.einsum('nbhqk,nbhkd->nbhqd', scores, vc)
    q_decay = jnp.exp(log_gamma[:, None] * (idx[None, :] + 1.0))
    k_decay = jnp.exp(log_gamma[:, None] * (C - 1.0 - idx[None, :]))
    chunk_decay = jnp.exp(log_gamma * C)

    def step(state, inp):
        kc_i, vc_i = inp
        new = chunk_decay[None, :, None, None] * state + jnp.einsum(
            'bhkd,bhke->bhde', kc_i * k_decay[None, :, :, None], vc_i)
        return new, state

    state0 = jnp.zeros((bsz, H, dh, dh), F32)
    _, prev_states = lax.scan(step, state0, (kc, vc))
    cross = jnp.einsum('nbhqd,nbhde->nbhqe', qc, prev_states) * q_decay[None, None, :, :, None]
    o = (inner + cross).transpose(1, 0, 3, 2, 4).reshape(bsz, seq, H, dh)
    mu = jnp.mean(o, -1, keepdims=True)
    var = jnp.mean(jnp.square(o - mu), -1, keepdims=True)
    o = (o - mu) * lax.rsqrt(var + LN_EPS) * gn_w.astype(F32).reshape(H, dh) + gn_b.astype(F32).reshape(H, dh)
    out = jax.nn.silu(g.astype(F32)) * o.reshape(bsz, seq, RET_WIDTH)
    return out.astype(q.dtype)


def stick_breaking_mixer(q, k, v, norm_w):
    bsz, seq, _ = q.shape
    H, dh, Bq = SB_HEADS, SB_HEAD_DIM, SB_BLOCK
    nb = seq // Bq
    scale = dh ** -0.5

    def heads(t):
        return t.astype(F32).reshape(bsz, seq, H, dh).transpose(0, 2, 1, 3)

    qh, kh, vh = heads(q), heads(k), heads(v)
    idx = jnp.arange(Bq)
    rev_incl = (idx[:, None] >= idx[None, :]).astype(F32)
    outs = []
    for bi in range(nb):
        L = (bi + 1) * Bq
        nkb = bi + 1
        qb = qh[:, :, bi * Bq:L]
        kb = kh[:, :, :L]
        vb = vh[:, :, :L]
        z = jnp.einsum('bhqd,bhkd->bhqk', qb, kb) * scale
        mask = jnp.arange(L)[None, :] < (bi * Bq + idx)[:, None]
        log_beta = jax.nn.log_sigmoid(z)
        log_1m = jnp.where(mask, log_beta - z, 0.0)
        blk = log_1m.reshape(bsz, H, Bq, nkb, Bq)
        within = jnp.einsum('bhqmj,js->bhqms', blk, rev_incl, precision=HI)
        later_mat = (jnp.arange(nkb)[:, None] > jnp.arange(nkb)[None, :]).astype(F32)
        later = jnp.einsum('bhqn,nm->bhqm', within[..., 0], later_mat, precision=HI)
        after = (within + later[..., None]).reshape(bsz, H, Bq, L) - log_1m
        a = jnp.where(mask, jnp.exp(log_beta + after), 0.0)
        outs.append(jnp.einsum('bhqk,bhkd->bhqd', a, vb))
    o = jnp.concatenate(outs, axis=2)
    o = o.transpose(0, 2, 1, 3)
    o = head_rms_norm(o, norm_w.reshape(H, dh))
    return o.reshape(bsz, seq, SB_WIDTH).astype(q.dtype)


def swiglu(x, w_gate, w_up, w_down):
    return (jax.nn.silu(x @ w_gate) * (x @ w_up)) @ w_down


def moe_swiglu(x, w_router, e_gate, e_up, e_down):
    bsz, seq, d = x.shape
    xt = x.reshape(-1, d)
    logits = (xt @ w_router).astype(F32)
    top_val, top_idx = lax.top_k(logits, TOP_K)
    top_w = jax.nn.softmax(top_val, axis=-1)
    gates = jnp.sum(jax.nn.one_hot(top_idx, N_EXPERTS, dtype=F32) * top_w[..., None], axis=1)
    out = jnp.zeros(xt.shape, F32)
    for e in range(N_EXPERTS):
        out = out + gates[:, e:e + 1] * swiglu(xt, e_gate[e], e_up[e], e_down[e]).astype(F32)
    return out.astype(x.dtype).reshape(bsz, seq, d)


def setup_inputs(seed: int = 0) -> dict:
    key = jax.random.key(seed)
    ks = iter(jax.random.split(key, 40))
    nrm = lambda shape, s: jax.random.normal(next(ks), shape, F32) * s
    G, H, P = SSM_GROUPS, SSM_GROUP_CH, SSM_STATE
    d_in = D_MODEL ** -0.5
    lam_im = jnp.broadcast_to(jnp.pi * jnp.arange(P, dtype=F32), (DEPTH, G, P)) + nrm((DEPTH, G, P), 0.01)
    return {
        "x": nrm((BATCH, SEQ, D_MODEL), 1.0),
        "w_in": nrm((DEPTH, D_MODEL, IN_COLS), d_in),
        "ssm_lam_re": -0.5 + nrm((DEPTH, G, P), 0.01),
        "ssm_lam_im": lam_im,
        "ssm_log_dt": jax.random.uniform(next(ks), (DEPTH, G), F32, math.log(0.001), math.log(0.1)),
        "ssm_b_re": nrm((DEPTH, G, P, H), (2.0 * H) ** -0.5),
        "ssm_b_im": nrm((DEPTH, G, P, H), (2.0 * H) ** -0.5),
        "ssm_c_re": nrm((DEPTH, G, H, P), (2.0 * P) ** -0.5),
        "ssm_c_im": nrm((DEPTH, G, H, P), (2.0 * P) ** -0.5),
        "ssm_d": nrm((DEPTH, SSM_WIDTH), 1.0),
        "ssm_glu_w": nrm((DEPTH, SSM_WIDTH, SSM_WIDTH), SSM_WIDTH ** -0.5),
        "ssm_glu_b": nrm((DEPTH, SSM_WIDTH), 0.01),
        "ssm_norm_w": 1.0 + nrm((DEPTH, SSM_WIDTH), 0.02),
        "ret_gn_w": 1.0 + nrm((DEPTH, RET_WIDTH), 0.02),
        "ret_gn_b": nrm((DEPTH, RET_WIDTH), 0.01),
        "sb_norm_w": 1.0 + nrm((DEPTH, SB_WIDTH), 0.02),
        "w_out": nrm((DEPTH, MIX_WIDTH, D_MODEL), MIX_WIDTH ** -0.5 * BETA_INIT),
        "ln_mix_w": 1.0 + nrm((DEPTH, D_MODEL), 0.02),
        "ln_mix_b": nrm((DEPTH, D_MODEL), 0.01),
        "ffn_w_gate": nrm((N_DENSE, D_MODEL, D_FF), d_in),
        "ffn_w_up": nrm((N_DENSE, D_MODEL, D_FF), d_in),
        "ffn_w_down": nrm((N_DENSE, D_FF, D_MODEL), D_FF ** -0.5 * BETA_INIT),
        "moe_router": nrm((N_MOE, D_MODEL, N_EXPERTS), d_in),
        "moe_w_gate": nrm((N_MOE, N_EXPERTS, D_MODEL, D_FF_EXPERT), d_in),
        "moe_w_up": nrm((N_MOE, N_EXPERTS, D_MODEL, D_FF_EXPERT), d_in),
        "moe_w_down": nrm((N_MOE, N_EXPERTS, D_FF_EXPERT, D_MODEL), D_FF_EXPERT ** -0.5 * BETA_INIT),
        "ln_ffn_w": 1.0 + nrm((DEPTH, D_MODEL), 0.02),
        "ln_ffn_b": nrm((DEPTH, D_MODEL), 0.01),
    }


def reference(x, w_in, ssm_lam_re, ssm_lam_im, ssm_log_dt, ssm_b_re, ssm_b_im, ssm_c_re, ssm_c_im,
              ssm_d, ssm_glu_w, ssm_glu_b, ssm_norm_w, ret_gn_w, ret_gn_b, sb_norm_w, w_out,
              ln_mix_w, ln_mix_b, ffn_w_gate, ffn_w_up, ffn_w_down, moe_router, moe_w_gate,
              moe_w_up, moe_w_down, ln_ffn_w, ln_ffn_b):
    sizes = (SSM_WIDTH,) + (RET_WIDTH,) * 4 + (SB_WIDTH,) * 3
    split_points = np.cumsum(sizes)[:-1].tolist()
    for layer in range(DEPTH):
        proj = x @ w_in[layer]
        u, rq, rk, rv, rg, sq, sk, sv = jnp.split(proj, split_points, axis=-1)
        y_ssm = s5_mixer(u, ssm_lam_re[layer], ssm_lam_im[layer], ssm_log_dt[layer],
                         ssm_b_re[layer], ssm_b_im[layer], ssm_c_re[layer], ssm_c_im[layer],
                         ssm_d[layer], ssm_glu_w[layer], ssm_glu_b[layer], ssm_norm_w[layer])
        y_ret = retention_mixer(rq, rk, rv, rg, ret_gn_w[layer], ret_gn_b[layer])
        y_sb = stick_breaking_mixer(sq, sk, sv, sb_norm_w[layer])
        mix = jnp.concatenate([y_ssm, y_ret, y_sb], axis=-1) @ w_out[layer]
        x = layer_norm(ALPHA * x + mix, ln_mix_w[layer], ln_mix_b[layer])
        li = layer // 2
        if layer % 2 == 0:
            f = swiglu(x, ffn_w_gate[li], ffn_w_up[li], ffn_w_down[li])
        else:
            f = moe_swiglu(x, moe_router[li], moe_w_gate[li], moe_w_up[li], moe_w_down[li])
        x = layer_norm(ALPHA * x + f, ln_ffn_w[layer], ln_ffn_b[layer])
    return x
```

```python
import functools
import math

import jax
import jax.numpy as jnp
from jax import lax
from jax.experimental import pallas as pl
from jax.experimental.pallas import tpu as pltpu

F32 = jnp.float32
BF16 = jnp.bfloat16

SSM_GROUP_CH = 16
SSM_STATE = 64
RET_HEAD_DIM = 64
SB_HEAD_DIM = 64
RET_CHUNK = 128
ROPE_BASE = 10000.0
TOP_K = 2
LN_EPS = 1e-5

LANES = 128
MXU_DIM = 256
VMEM_LIMIT_BYTES = 56 * 1024 * 1024

S5_CHUNK = 16
S5_PAIR_CH = 2 * SSM_GROUP_CH
SB_TILE = 256
SB_EXIT_LOG = 110.0


def _cparams(sem, vmem=VMEM_LIMIT_BYTES):
    return pltpu.CompilerParams(dimension_semantics=sem, vmem_limit_bytes=vmem)


def _resident(shape):
    nd = len(shape)
    return pl.BlockSpec(shape, lambda *_: (0,) * nd)


def _layer_norm_rows(h, w, b):
    mu = jnp.mean(h, axis=-1, keepdims=True)
    d = h - mu
    var = jnp.mean(d * d, axis=-1, keepdims=True)
    return d * lax.rsqrt(var + LN_EPS) * w + b


def _split_bf16(v):
    hi = v.astype(BF16)
    lo = (v - hi.astype(F32)).astype(BF16)
    return hi, lo


def _proj_kernel(x_ref, w_ref, u_ref, ub_ref, rq_ref, rk_ref, rv_ref, rg_ref,
                 sq_ref, sk_ref, sv_ref, *, ssm_w, ret_w, sb_w):
    xb = x_ref[...].astype(BF16)
    col = [0]

    def mm(width):
        c0 = col[0]
        col[0] = c0 + width
        return jnp.dot(xb, w_ref[:, c0:c0 + width], preferred_element_type=F32)

    u = mm(ssm_w)
    u_ref[...] = u
    ub_ref[...] = u.astype(BF16)
    rq_ref[...] = mm(ret_w)
    rk_ref[...] = mm(ret_w)
    rv_ref[...] = mm(ret_w).astype(BF16)
    rg_ref[...] = mm(ret_w)
    sq_ref[...] = mm(sb_w).astype(BF16)
    sk_ref[...] = mm(sb_w).astype(BF16)
    sv_ref[...] = mm(sb_w).astype(BF16)


def _proj(x2, w_bf, ssm_w, ret_w, sb_w, tm):
    t, d = x2.shape
    row = lambda w: pl.BlockSpec((tm, w), lambda i: (i, 0))
    out_shape = (
        jax.ShapeDtypeStruct((t, ssm_w), F32), jax.ShapeDtypeStruct((t, ssm_w), BF16),
        jax.ShapeDtypeStruct((t, ret_w), F32), jax.ShapeDtypeStruct((t, ret_w), F32),
        jax.ShapeDtypeStruct((t, ret_w), BF16), jax.ShapeDtypeStruct((t, ret_w), F32),
        jax.ShapeDtypeStruct((t, sb_w), BF16), jax.ShapeDtypeStruct((t, sb_w), BF16),
        jax.ShapeDtypeStruct((t, sb_w), BF16),
    )
    out_specs = (row(ssm_w), row(ssm_w), row(ret_w), row(ret_w), row(ret_w), row(ret_w),
                 row(sb_w), row(sb_w), row(sb_w))
    return pl.pallas_call(
        functools.partial(_proj_kernel, ssm_w=ssm_w, ret_w=ret_w, sb_w=sb_w),
        grid=(t // tm,),
        in_specs=[row(d), _resident(w_bf.shape)],
        out_specs=out_specs,
        out_shape=out_shape,
        compiler_params=_cparams(("parallel",)),
        name="proj",
    )(x2, w_bf)


def _s5_tables(lam_re, lam_im, log_dt, b_re, b_im, c_re, c_im):
    hi = lax.Precision.HIGHEST
    g, p = lam_re.shape
    lc = S5_CHUNK
    pairs = g // 2
    dt = jnp.exp(log_dt)[:, None]
    a_re, a_im = lam_re * dt, lam_im * dt

    def lam_pow(n):
        n = n[:, None, None]
        mag = jnp.exp(n * a_re)
        return mag * jnp.cos(n * a_im), mag * jnp.sin(n * a_im)

    lb_re, lb_im = lam_pow(jnp.ones((1,), F32))
    lb_re, lb_im = lb_re[0], lb_im[0]
    den = lam_re * lam_re + lam_im * lam_im
    z_re = ((lb_re - 1.0) * lam_re + lb_im * lam_im) / den
    z_im = (lb_im * lam_re - (lb_re - 1.0) * lam_im) / den
    bb_re = z_re[..., None] * b_re - z_im[..., None] * b_im
    bb_im = z_re[..., None] * b_im + z_im[..., None] * b_re

    steps = jnp.arange(lc, dtype=F32)
    pw_re, pw_im = lam_pow(steps)
    cl_re = c_re[None] * pw_re[:, :, None, :] - c_im[None] * pw_im[:, :, None, :]
    cl_im = c_re[None] * pw_im[:, :, None, :] + c_im[None] * pw_re[:, :, None, :]
    kern = (jnp.einsum('lghp,gpk->lghk', cl_re, bb_re, precision=hi)
            - jnp.einsum('lghp,gpk->lghk', cl_im, bb_im, precision=hi))
    lag = jnp.arange(lc)[None, :] - jnp.arange(lc)[:, None]
    kfull = jnp.where((lag >= 0)[:, :, None, None, None], kern[jnp.maximum(lag, 0)], 0.0)
    eye2 = jnp.eye(2, dtype=F32)
    hch = b_re.shape[-1]
    kfull = kfull.reshape(lc, lc, pairs, 2, hch, hch)
    mt = jnp.einsum('stkghj,gf->ksgjtfh', kfull, eye2).reshape(pairs, lc * 2 * hch, lc * 2 * hch)

    rw_re, rw_im = lam_pow(lc - 1.0 - steps)
    w_re = rw_re[..., None] * bb_re[None] - rw_im[..., None] * bb_im[None]
    w_im = rw_re[..., None] * bb_im[None] + rw_im[..., None] * bb_re[None]
    w_ri = jnp.stack([w_re, w_im]).reshape(2, lc, pairs, 2, p, hch)
    wt = jnp.einsum('rskgpj,gf->ksgjrfp', w_ri, eye2).reshape(pairs, lc * 2 * hch, 2 * 2 * p)

    ow_re, ow_im = lam_pow(steps + 1.0)
    v_re = c_re[None] * ow_re[:, :, None, :] - c_im[None] * ow_im[:, :, None, :]
    v_im = -(c_re[None] * ow_im[:, :, None, :] + c_im[None] * ow_re[:, :, None, :])
    v_ri = jnp.stack([v_re, v_im]).reshape(2, lc, pairs, 2, hch, p)
    vt = jnp.einsum('rtkghp,gf->krgptfh', v_ri, eye2).reshape(pairs, 2 * 2 * p, lc * 2 * hch)

    lc_re, lc_im = lam_pow(jnp.full((1,), float(lc), F32))
    lam_chunk = jnp.stack([lc_re[0].reshape(pairs, 2 * p), lc_im[0].reshape(pairs, 2 * p)], axis=1)
    return mt.astype(BF16), wt.astype(BF16), vt.astype(BF16), lam_chunk.reshape(1, pairs * 4 * p)


def _s5_state_kernel(a_ref, wt_ref, lam_ref, xp_ref, carry_ref, sin_ref, xpf_ref, *, pairs, rows):
    @pl.when(pl.program_id(1) == 0)
    def _():
        carry_ref[...] = jnp.zeros_like(carry_ref)

    blk = 2 * LANES
    for k in range(pairs):
        sin_ref[:, k * blk:(k + 1) * blk] = jnp.dot(a_ref[k], wt_ref[k], preferred_element_type=F32)

    lam = lam_ref[...]
    l_re = [lam[:, k * blk:k * blk + LANES] for k in range(pairs)]
    l_im = [lam[:, k * blk + LANES:(k + 1) * blk] for k in range(pairs)]

    def step(r, x):
        xpf_ref[pl.ds(r, 1), :] = x
        s = sin_ref[pl.ds(r, 1), :]
        parts = []
        for k in range(pairs):
            xr = x[:, k * blk:k * blk + LANES]
            xi = x[:, k * blk + LANES:(k + 1) * blk]
            parts.append(l_re[k] * xr - l_im[k] * xi + s[:, k * blk:k * blk + LANES])
            parts.append(l_re[k] * xi + l_im[k] * xr + s[:, k * blk + LANES:(k + 1) * blk])
        return jnp.concatenate(parts, axis=1)

    carry_ref[...] = lax.fori_loop(0, rows, step, carry_ref[...])
    xp_ref[...] = xpf_ref[...].astype(BF16)


def _s5_states(a, wt, lam_chunk, batch, rows):
    pairs, n, kdim = a.shape
    per_b = n // batch // rows
    width = lam_chunk.shape[1]
    return pl.pallas_call(
        functools.partial(_s5_state_kernel, pairs=pairs, rows=rows),
        grid=(batch, per_b),
        in_specs=[pl.BlockSpec((pairs, rows, kdim), lambda b, i: (0, b * per_b + i, 0)),
                  _resident(wt.shape), _resident(lam_chunk.shape)],
        out_specs=pl.BlockSpec((rows, width), lambda b, i: (b * per_b + i, 0)),
        out_shape=jax.ShapeDtypeStruct((n, width), BF16),
        scratch_shapes=[pltpu.VMEM((1, width), F32), pltpu.VMEM((rows, width), F32),
                        pltpu.VMEM((rows, width), F32)],
        compiler_params=_cparams(("arbitrary", "arbitrary")),
        name="s5_states",
    )(a, wt, lam_chunk)


def _s5_out_kernel(a_ref, mt_ref, xp_ref, vt_ref, y_ref):
    y_ref[...] = (jnp.dot(a_ref[...], mt_ref[...], preferred_element_type=F32)
                  + jnp.dot(xp_ref[...], vt_ref[...], preferred_element_type=F32))


def _s5_outputs(a, mt, xp, vt, rows):
    pairs, n, kdim = a.shape
    sdim = vt.shape[1]
    return pl.pallas_call(
        _s5_out_kernel,
        grid=(pairs, n // rows),
        in_specs=[pl.BlockSpec((None, rows, kdim), lambda k, i: (k, i, 0)),
                  pl.BlockSpec((None, kdim, kdim), lambda k, i: (k, 0, 0)),
                  pl.BlockSpec((rows, sdim), lambda k, i: (i, k)),
                  pl.BlockSpec((None, sdim, kdim), lambda k, i: (k, 0, 0))],
        out_specs=pl.BlockSpec((None, rows, kdim), lambda k, i: (k, i, 0)),
        out_shape=jax.ShapeDtypeStruct((pairs, n, kdim), F32),
        compiler_params=_cparams(("parallel", "parallel")),
        name="s5_outputs",
    )(a, mt, xp, vt)


def _s5_post_kernel(y_ref, u_ref, d_ref, gw_ref, gb_ref, nw_ref, gm_ref, o_ref):
    y = y_ref[...] + d_ref[...] * u_ref[...]
    g = y * (0.5 * (1.0 + jnp.tanh(math.sqrt(2.0 / math.pi) * (y + 0.044715 * (y * y * y)))))
    z = jnp.dot(g.astype(BF16), gw_ref[...], preferred_element_type=F32) + gb_ref[...]
    out = g * (1.0 / (1.0 + jnp.exp(-z)))
    hi, lo = _split_bf16(out * out)
    ms = (jnp.dot(hi, gm_ref[...], preferred_element_type=F32)
          + jnp.dot(lo, gm_ref[...], preferred_element_type=F32))
    o_ref[...] = (out * lax.rsqrt(ms + LN_EPS) * nw_ref[...]).astype(BF16)


def _s5_post(y, u, d_skip, glu_w, glu_b, norm_w, tm):
    t, w = y.shape
    groups = jnp.arange(w) // SSM_GROUP_CH
    gmean = jnp.where(groups[:, None] == groups[None, :], 1.0 / SSM_GROUP_CH, 0.0).astype(BF16)
    row = pl.BlockSpec((tm, w), lambda i: (i, 0))
    vec = _resident((1, w))
    return pl.pallas_call(
        _s5_post_kernel,
        grid=(t // tm,),
        in_specs=[row, row, vec, _resident((w, w)), vec, vec, _resident((w, w))],
        out_specs=row,
        out_shape=jax.ShapeDtypeStruct((t, w), BF16),
        compiler_params=_cparams(("parallel",)),
        name="s5_post",
    )(y, u, d_skip.reshape(1, w), glu_w.astype(BF16), glu_b.reshape(1, w), norm_w.reshape(1, w), gmean)


def _s5_mixer(u, ub, batch, lam_re, lam_im, log_dt, b_re, b_im, c_re, c_im, d_skip, glu_w, glu_b, norm_w, tm):
    t, w = u.shape
    lc = S5_CHUNK
    n = t // lc
    pairs = w // S5_PAIR_CH
    mt, wt, vt, lam_chunk = _s5_tables(lam_re, lam_im, log_dt, b_re, b_im, c_re, c_im)
    a = ub.reshape(n, lc, pairs, S5_PAIR_CH).transpose(2, 0, 1, 3).reshape(pairs, n, lc * S5_PAIR_CH)
    rows = min(256, n // batch)
    xp = _s5_states(a, wt, lam_chunk, batch, rows)
    y = _s5_outputs(a, mt, xp, vt, min(512, n))
    y = y.reshape(pairs, n, lc, S5_PAIR_CH).transpose(1, 2, 0, 3).reshape(t, w)
    return _s5_post(y, u, d_skip, glu_w, glu_b, norm_w, tm)


def _retention_kernel(q_ref, k_ref, v_ref, g_ref, cos_ref, sin_ref, qdec_ref, kdec_ref, dmask_ref,
                      cdec_ref, gnw_ref, gnb_ref, gm_ref, bmask_ref, o_ref, state_ref, *, heads):
    c = q_ref.shape[0]
    width = q_ref.shape[1]
    half_w = MXU_DIM
    halves = width // half_w
    heads_per_half = heads // halves
    dh = width // heads

    @pl.when(pl.program_id(1) == 0)
    def _():
        state_ref[...] = jnp.zeros_like(state_ref)

    lane = lax.broadcasted_iota(jnp.int32, (c, width), 1)
    first_half = (lane % dh) < (dh // 2)

    def rope(x):
        swapped = jnp.where(first_half, pltpu.roll(x, width - dh // 2, axis=1), pltpu.roll(x, dh // 2, axis=1))
        return x * cos_ref[...] + swapped * sin_ref[...]

    q = rope(q_ref[...])
    k = rope(k_ref[...]) * (dh ** -0.5)
    v = v_ref[...]
    qb = q.astype(BF16)
    kb = k.astype(BF16)
    qd = (q * qdec_ref[...]).astype(BF16)
    kd = k * kdec_ref[...]
    lane_h = lax.broadcasted_iota(jnp.int32, (c, half_w), 1) // dh

    outs = []
    for hf in range(halves):
        sl = slice(hf * half_w, (hf + 1) * half_w)
        qh, kh, vh = qb[:, sl], kb[:, sl], v[:, sl]
        scores, vstack = [], []
        for j in range(heads_per_half):
            sel = lane_h == j
            s = lax.dot_general(jnp.where(sel, qh, jnp.zeros_like(qh)), kh, (((1,), (1,)), ((), ())),
                                preferred_element_type=F32)
            scores.append((s * dmask_ref[hf * heads_per_half + j]).astype(BF16))
            vstack.append(jnp.where(sel, vh, jnp.zeros_like(vh)))
        inner = jnp.dot(jnp.concatenate(scores, axis=1), jnp.concatenate(vstack, axis=0),
                        preferred_element_type=F32)
        st = state_ref[hf]
        cross = jnp.dot(qd[:, sl], st.astype(BF16), preferred_element_type=F32)
        outs.append(inner + cross)
        upd = jnp.dot(kd[:, sl].T.astype(BF16), vh, preferred_element_type=F32)
        state_ref[hf] = st * cdec_ref[:, sl] + upd * bmask_ref[...]
    o = jnp.concatenate(outs, axis=1)

    def group_mean(a):
        hi, lo = _split_bf16(a)
        parts = []
        for hf in range(halves):
            sl = slice(hf * half_w, (hf + 1) * half_w)
            parts.append(jnp.dot(hi[:, sl], gm_ref[...], preferred_element_type=F32)
                         + jnp.dot(lo[:, sl], gm_ref[...], preferred_element_type=F32))
        return jnp.concatenate(parts, axis=1)

    d = o - group_mean(o)
    var = group_mean(d * d)
    on = d * lax.rsqrt(var + LN_EPS) * gnw_ref[...] + gnb_ref[...]
    g = g_ref[...]
    o_ref[...] = (g * (1.0 / (1.0 + jnp.exp(-g))) * on).astype(BF16)


def _retention_mixer(rq, rk, rv, rg, gn_w, gn_b, batch, seq):
    t, width = rq.shape
    heads = width // RET_HEAD_DIM
    dh = RET_HEAD_DIM
    c = RET_CHUNK
    n = seq // c
    half = dh // 2
    pos = jnp.arange(seq, dtype=F32)
    inv = ROPE_BASE ** (-jnp.arange(half, dtype=F32) / half)
    ang = pos[:, None] * inv[None, :]
    cos_t = jnp.tile(jnp.concatenate([jnp.cos(ang), jnp.cos(ang)], axis=1), (1, heads))
    sin_t = jnp.tile(jnp.concatenate([-jnp.sin(ang), jnp.sin(ang)], axis=1), (1, heads))
    log_gamma = jnp.log(1.0 - 2.0 ** (-5.0 - jnp.arange(heads, dtype=F32)))
    idx = jnp.arange(c, dtype=F32)
    rel = idx[:, None] - idx[None, :]
    dmask = jnp.where(rel >= 0, jnp.exp(log_gamma[:, None, None] * jnp.maximum(rel, 0.0)), 0.0)
    per_lane = lambda a: jnp.repeat(a, dh, axis=-1)
    qdec = per_lane(jnp.exp(log_gamma[None, :] * (idx[:, None] + 1.0)))
    kdec = per_lane(jnp.exp(log_gamma[None, :] * (c - 1.0 - idx[:, None])))
    cdec = per_lane(jnp.exp(log_gamma * c)[None, :])
    hid = jnp.arange(MXU_DIM) // dh
    same = hid[:, None] == hid[None, :]
    gmean = jnp.where(same, 1.0 / dh, 0.0).astype(BF16)
    bmask = same.astype(F32)

    row = pl.BlockSpec((c, width), lambda b, i: (b * n + i, 0))
    tab = pl.BlockSpec((c, width), lambda b, i: (i, 0))
    halves = width // MXU_DIM
    return pl.pallas_call(
        functools.partial(_retention_kernel, heads=heads),
        grid=(batch, n),
        in_specs=[row, row, row, row, tab, tab, _resident((c, width)), _resident((c, width)),
                  _resident((heads, c, c)), _resident((1, width)), _resident((1, width)),
                  _resident((1, width)), _resident((MXU_DIM, MXU_DIM)), _resident((MXU_DIM, MXU_DIM))],
        out_specs=row,
        out_shape=jax.ShapeDtypeStruct((t, width), BF16),
        scratch_shapes=[pltpu.VMEM((halves, MXU_DIM, MXU_DIM), F32)],
        compiler_params=_cparams(("arbitrary", "arbitrary")),
        name="retention",
    )(rq, rk, rv, rg, cos_t, sin_t, qdec, kdec, dmask, cdec, gn_w.reshape(1, width),
      gn_b.reshape(1, width), gmean, bmask)


def _sb_kernel(q_ref, k_ref, v_ref, nw_ref, tri_ref, o_ref, acc_ref, run_ref, *, heads):
    tq, width = q_ref.shape
    tk = tq
    dh = width // heads
    scale = dh ** -0.5
    i = pl.program_id(1)
    q = q_ref[...]
    lane_h = lax.broadcasted_iota(jnp.int32, (tq, width), 1) // dh
    rows = lax.broadcasted_iota(jnp.int32, (tq, tk), 0)
    cols = lax.broadcasted_iota(jnp.int32, (tq, tk), 1)
    causal = cols < rows
    tri = tri_ref[...]

    def tile(qm, j, masked):
        start = pl.multiple_of(j * tk, tk)
        kb = k_ref[pl.ds(start, tk), :]
        vb = v_ref[pl.ds(start, tk), :]
        z = lax.dot_general(qm, kb, (((1,), (1,)), ((), ())), preferred_element_type=F32) * scale
        log_beta = jnp.minimum(z, 0.0) - jnp.log1p(jnp.exp(-jnp.abs(z)))
        log_1m = log_beta - z
        if masked:
            log_1m = jnp.where(causal, log_1m, 0.0)
        hi, lo = _split_bf16(log_1m)
        after = (jnp.dot(hi, tri, preferred_element_type=F32) + jnp.dot(lo, tri, preferred_element_type=F32))
        run = run_ref[...]
        a = jnp.exp(log_beta + after + run)
        if masked:
            a = jnp.where(causal, a, 0.0)
        acc_ref[...] += jnp.dot(a.astype(BF16), vb, preferred_element_type=F32)
        run = run + jnp.sum(log_1m, axis=1, keepdims=True)
        run_ref[...] = run
        return jnp.max(run)

    out = jnp.zeros((tq, width), F32)
    for h in range(heads):
        sel = lane_h == h
        qm = jnp.where(sel, q, jnp.zeros_like(q))
        acc_ref[...] = jnp.zeros_like(acc_ref)
        run_ref[...] = jnp.zeros_like(run_ref)
        top = tile(qm, i, True)

        def cond(carry):
            j, top = carry
            return jnp.logical_and(j >= 0, top > -SB_EXIT_LOG)

        def body(carry):
            j, _ = carry
            return j - 1, tile(qm, j, False)

        lax.while_loop(cond, body, (i - 1, top))
        o = acc_ref[...]
        ms = jnp.sum(jnp.where(sel, o * o, 0.0), axis=1, keepdims=True) * (1.0 / dh)
        out = out + jnp.where(sel, o * lax.rsqrt(ms + LN_EPS), 0.0)
    o_ref[...] = (out * nw_ref[...]).astype(BF16)


def _sb_mixer(sq, sk, sv, norm_w, batch, seq):
    t, width = sq.shape
    heads = width // SB_HEAD_DIM
    tq = min(SB_TILE, seq)
    nq = seq // tq
    idx = jnp.arange(tq)
    tri = (idx[:, None] > idx[None, :]).astype(BF16)
    whole = pl.BlockSpec((None, seq, width), lambda b, i: (b, 0, 0))
    row = pl.BlockSpec((tq, width), lambda b, i: (b * nq + i, 0))
    return pl.pallas_call(
        functools.partial(_sb_kernel, heads=heads),
        grid=(batch, nq),
        in_specs=[row, whole, whole, _resident((1, width)), _resident((tq, tq))],
        out_specs=row,
        out_shape=jax.ShapeDtypeStruct((t, width), BF16),
        scratch_shapes=[pltpu.VMEM((tq, width), F32), pltpu.VMEM((tq, 1), F32)],
        compiler_params=_cparams(("parallel", "arbitrary")),
        name="stick_breaking",
    )(sq, sk.reshape(batch, seq, width), sv.reshape(batch, seq, width), norm_w.reshape(1, width), tri)


def _mix_kernel(x_ref, ya_ref, yb_ref, yc_ref, w_ref, lw_ref, lb_ref, o_ref, *, alpha):
    wa = ya_ref.shape[1]
    wb = yb_ref.shape[1]
    mix = (jnp.dot(ya_ref[...], w_ref[0:wa, :], preferred_element_type=F32)
           + jnp.dot(yb_ref[...], w_ref[wa:wa + wb, :], preferred_element_type=F32)
           + jnp.dot(yc_ref[...], w_ref[wa + wb:, :], preferred_element_type=F32))
    o_ref[...] = _layer_norm_rows(alpha * x_ref[...] + mix, lw_ref[...], lb_ref[...])


def _mix(x2, ya, yb, yc, w_bf, ln_w, ln_b, alpha, tm):
    t, d = x2.shape
    row = lambda w: pl.BlockSpec((tm, w), lambda i: (i, 0))
    return pl.pallas_call(
        functools.partial(_mix_kernel, alpha=alpha),
        grid=(t // tm,),
        in_specs=[row(d), row(ya.shape[1]), row(yb.shape[1]), row(yc.shape[1]), _resident(w_bf.shape),
                  _resident((1, d)), _resident((1, d))],
        out_specs=row(d),
        out_shape=jax.ShapeDtypeStruct((t, d), F32),
        compiler_params=_cparams(("parallel",)),
        name="mix_ln",
    )(x2, ya, yb, yc, w_bf, ln_w.reshape(1, d), ln_b.reshape(1, d))


def _swiglu_acc(xb, wg_ref, wu_ref, wd_ref, acc_ref, chunk, scale=None):
    ff = wg_ref.shape[-1]
    for c0 in range(0, ff, chunk):
        g = jnp.dot(xb, wg_ref[:, c0:c0 + chunk], preferred_element_type=F32)
        u = jnp.dot(xb, wu_ref[:, c0:c0 + chunk], preferred_element_type=F32)
        h = g * (1.0 / (1.0 + jnp.exp(-g))) * u
        if scale is not None:
            h = h * scale
        acc_ref[...] += jnp.dot(h.astype(BF16), wd_ref[c0:c0 + chunk, :], preferred_element_type=F32)


def _ffn_kernel(x_ref, wg_ref, wu_ref, wd_ref, lw_ref, lb_ref, o_ref, acc_ref, *, alpha):
    x = x_ref[...]
    acc_ref[...] = jnp.zeros_like(acc_ref)
    _swiglu_acc(x.astype(BF16), wg_ref, wu_ref, wd_ref, acc_ref, MXU_DIM)
    o_ref[...] = _layer_norm_rows(alpha * x + acc_ref[...], lw_ref[...], lb_ref[...])


def _pad_to(a, axis, mult):
    size = a.shape[axis]
    pad = (-size) % mult
    if pad == 0:
        return a
    widths = [(0, 0)] * a.ndim
    widths[axis] = (0, pad)
    return jnp.pad(a, widths)


def _ffn(x2, w_gate, w_up, w_down, ln_w, ln_b, alpha, tm):
    t, d = x2.shape
    wg = _pad_to(w_gate, 1, MXU_DIM).astype(BF16)
    wu = _pad_to(w_up, 1, MXU_DIM).astype(BF16)
    wd = _pad_to(w_down, 0, MXU_DIM).astype(BF16)
    row = pl.BlockSpec((tm, d), lambda i: (i, 0))
    return pl.pallas_call(
        functools.partial(_ffn_kernel, alpha=alpha),
        grid=(t // tm,),
        in_specs=[row, _resident(wg.shape), _resident(wu.shape), _resident(wd.shape),
                  _resident((1, d)), _resident((1, d))],
        out_specs=row,
        out_shape=jax.ShapeDtypeStruct((t, d), F32),
        scratch_shapes=[pltpu.VMEM((tm, d), F32)],
        compiler_params=_cparams(("parallel",)),
        name="ffn_ln",
    )(x2, wg, wu, wd, ln_w.reshape(1, d), ln_b.reshape(1, d))


def _router_kernel(x_ref, wh_ref, wl_ref, g_ref, *, n_experts):
    xh, xl = _split_bf16(x_ref[...])
    logits = (jnp.dot(xh, wh_ref[...], preferred_element_type=F32)
              + jnp.dot(xl, wh_ref[...], preferred_element_type=F32)
              + jnp.dot(xh, wl_ref[...], preferred_element_type=F32))
    lane = lax.broadcasted_iota(jnp.int32, logits.shape, 1).astype(F32)
    neg = -1e30
    logits = jnp.where(lane < n_experts, logits, neg)
    m1 = jnp.max(logits, axis=1, keepdims=True)
    i1 = jnp.min(jnp.where(logits == m1, lane, float(LANES)), axis=1, keepdims=True)
    rest = jnp.where(lane == i1, neg, logits)
    m2 = jnp.max(rest, axis=1, keepdims=True)
    i2 = jnp.min(jnp.where(rest == m2, lane, float(LANES)), axis=1, keepdims=True)
    e2 = jnp.exp(m2 - m1)
    w1 = 1.0 / (1.0 + e2)
    g_ref[...] = jnp.where(lane == i1, w1, 0.0) + jnp.where(lane == i2, e2 * w1, 0.0)


def _router(x2, w_router, tm):
    t, d = x2.shape
    n_experts = w_router.shape[1]
    wp = _pad_to(w_router, 1, LANES)
    wh, wl = _split_bf16(wp)
    row = pl.BlockSpec((tm, d), lambda i: (i, 0))
    return pl.pallas_call(
        functools.partial(_router_kernel, n_experts=n_experts),
        grid=(t // tm,),
        in_specs=[row, _resident(wh.shape), _resident(wl.shape)],
        out_specs=pl.BlockSpec((tm, LANES), lambda i: (i, 0)),
        out_shape=jax.ShapeDtypeStruct((t, LANES), F32),
        compiler_params=_cparams(("parallel",)),
        name="router",
    )(x2, wh, wl)


def _moe_kernel(x_ref, gates_ref, wg_ref, wu_ref, wd_ref, lw_ref, lb_ref, o_ref, acc_ref, *, alpha):
    e = pl.program_id(1)

    @pl.when(e == 0)
    def _():
        acc_ref[...] = jnp.zeros_like(acc_ref)

    gates = gates_ref[...]
    lane = lax.broadcasted_iota(jnp.int32, gates.shape, 1)
    gate = jnp.sum(jnp.where(lane == e, gates, 0.0), axis=1, keepdims=True)
    x = x_ref[...]
    _swiglu_acc(x.astype(BF16), wg_ref, wu_ref, wd_ref, acc_ref, MXU_DIM, scale=gate)

    @pl.when(e == pl.num_programs(1) - 1)
    def _():
        o_ref[...] = _layer_norm_rows(alpha * x + acc_ref[...], lw_ref[...], lb_ref[...])


def _moe(x2, gates, e_gate, e_up, e_down, ln_w, ln_b, alpha, tm):
    t, d = x2.shape
    n_experts = e_gate.shape[0]
    wg = _pad_to(e_gate, 2, MXU_DIM).astype(BF16)
    wu = _pad_to(e_up, 2, MXU_DIM).astype(BF16)
    wd = _pad_to(e_down, 1, MXU_DIM).astype(BF16)
    ffp = wg.shape[2]
    row = pl.BlockSpec((tm, d), lambda i, e: (i, 0))
    return pl.pallas_call(
        functools.partial(_moe_kernel, alpha=alpha),
        grid=(t // tm, n_experts),
        in_specs=[row, pl.BlockSpec((tm, LANES), lambda i, e: (i, 0)),
                  pl.BlockSpec((None, d, ffp), lambda i, e: (e, 0, 0)),
                  pl.BlockSpec((None, d, ffp), lambda i, e: (e, 0, 0)),
                  pl.BlockSpec((None, ffp, d), lambda i, e: (e, 0, 0)),
                  _resident((1, d)), _resident((1, d))],
        out_specs=row,
        out_shape=jax.ShapeDtypeStruct((t, d), F32),
        scratch_shapes=[pltpu.VMEM((tm, d), F32)],
        compiler_params=_cparams(("parallel", "arbitrary")),
        name="moe_ln",
    )(x2, gates, wg, wu, wd, ln_w.reshape(1, d), ln_b.reshape(1, d))


def kernel(x, w_in, ssm_lam_re, ssm_lam_im, ssm_log_dt, ssm_b_re, ssm_b_im, ssm_c_re, ssm_c_im, ssm_d, ssm_glu_w, ssm_glu_b, ssm_norm_w, ret_gn_w, ret_gn_b, sb_norm_w, w_out, ln_mix_w, ln_mix_b, ffn_w_gate, ffn_w_up, ffn_w_down, moe_router, moe_w_gate, moe_w_up, moe_w_down, ln_ffn_w, ln_ffn_b):
    batch, seq, d = x.shape
    depth = w_in.shape[0]
    alpha = (2.0 * depth) ** 0.25
    ssm_w = ssm_d.shape[1]
    ret_w = ret_gn_w.shape[1]
    sb_w = sb_norm_w.shape[1]
    t = batch * seq
    tm = min(512, t)
    h = x.reshape(t, d)
    for layer in range(depth):
        u, ub, rq, rk, rv, rg, sq, sk, sv = _proj(h, w_in[layer].astype(BF16), ssm_w, ret_w, sb_w, tm)
        y_ssm = _s5_mixer(u, ub, batch, ssm_lam_re[layer], ssm_lam_im[layer], ssm_log_dt[layer],
                          ssm_b_re[layer], ssm_b_im[layer], ssm_c_re[layer], ssm_c_im[layer],
                          ssm_d[layer], ssm_glu_w[layer], ssm_glu_b[layer], ssm_norm_w[layer], tm)
        y_ret = _retention_mixer(rq, rk, rv, rg, ret_gn_w[layer], ret_gn_b[layer], batch, seq)
        y_sb = _sb_mixer(sq, sk, sv, sb_norm_w[layer], batch, seq)
        h = _mix(h, y_ssm, y_ret, y_sb, w_out[layer].astype(BF16), ln_mix_w[layer], ln_mix_b[layer], alpha, tm)
        li = layer // 2
        if layer % 2 == 0:
            h = _ffn(h, ffn_w_gate[li], ffn_w_up[li], ffn_w_down[li], ln_ffn_w[layer], ln_ffn_b[layer], alpha, tm)
        else:
            gates = _router(h, moe_router[li], tm)
            h = _moe(h, gates, moe_w_gate[li], moe_w_up[li], moe_w_down[li], ln_ffn_w[layer],
                     ln_ffn_b[layer], alpha, tm)
    return h.reshape(batch, seq, d)
```

```python
import functools
import math

import jax
import jax.numpy as jnp
from jax import lax
from jax.experimental import pallas as pl
from jax.experimental.pallas import tpu as pltpu

F32 = jnp.float32
BF16 = jnp.bfloat16

SSM_GROUP_CH = 16
SSM_STATE = 64
RET_HEAD_DIM = 64
SB_HEAD_DIM = 64
RET_CHUNK = 128
ROPE_BASE = 10000.0
TOP_K = 2
LN_EPS = 1e-5
LOG2_E = math.log2(math.e)

LANES = 128
SUBLANES = 8
MXU_DIM = 256
VMEM_LIMIT_BYTES = 56 * 1024 * 1024

ROW_TILE = 512
S5_CHUNK = 16
S5_PAIR_CH = 2 * SSM_GROUP_CH
S5_SUPER = 16
S5_ROWS = 256
RET_STEPS = 2
SB_TILE = 256
SB_EXIT_LOG = 110.0


def _cparams(sem, vmem=VMEM_LIMIT_BYTES):
    return pltpu.CompilerParams(dimension_semantics=sem, vmem_limit_bytes=vmem)


def _resident(shape):
    nd = len(shape)
    return pl.BlockSpec(shape, lambda *_: (0,) * nd)


def _layer_of(shape, layer):
    nd = len(shape)
    return pl.BlockSpec((None,) + tuple(shape[1:]), lambda *_: (layer,) + (0,) * (nd - 1))


def _layer_norm_rows(h, w, b):
    mu = jnp.mean(h, axis=-1, keepdims=True)
    d = h - mu
    var = jnp.mean(d * d, axis=-1, keepdims=True)
    return d * lax.rsqrt(var + LN_EPS) * w + b


def _split_bf16(v):
    hi = v.astype(BF16)
    lo = (v - hi.astype(F32)).astype(BF16)
    return hi, lo


def _dot_nt(a, b):
    return lax.dot_general(a, b, (((1,), (1,)), ((), ())), preferred_element_type=F32)


def _sigmoid(v):
    return 1.0 / (1.0 + jnp.exp(-v))


def _row_permutation(groups, per_group):
    n = groups * per_group
    src = jnp.arange(n)
    dst = (src % per_group) * groups + src // per_group
    return (dst[:, None] == jnp.arange(n)[None, :]).astype(BF16).T


def _proj_kernel(x_ref, w_ref, perm_ref, a_ref, rq_ref, rk_ref, rv_ref, rg_ref,
                 sq_ref, sk_ref, sv_ref, *, ssm_w, ret_w, sb_w):
    xb = x_ref[...].astype(BF16)
    col = [0]

    def mm(width):
        c0 = col[0]
        col[0] = c0 + width
        return jnp.dot(xb, w_ref[:, c0:c0 + width], preferred_element_type=F32)

    u = mm(ssm_w).astype(BF16)
    up = jnp.dot(perm_ref[...], u, preferred_element_type=F32).astype(BF16)
    nch = u.shape[0] // S5_CHUNK
    for k in range(ssm_w // S5_PAIR_CH):
        a_ref[k] = jnp.concatenate(
            [up[s * nch:(s + 1) * nch, k * S5_PAIR_CH:(k + 1) * S5_PAIR_CH] for s in range(S5_CHUNK)], axis=1)
    rq_ref[...] = mm(ret_w)
    rk_ref[...] = mm(ret_w)
    rv_ref[...] = mm(ret_w).astype(BF16)
    rg_ref[...] = mm(ret_w)
    sq_ref[...] = mm(sb_w).astype(BF16)
    sk_ref[...] = mm(sb_w).astype(BF16)
    sv_ref[...] = mm(sb_w).astype(BF16)


def _proj(x2, w_bf, layer, ssm_w, ret_w, sb_w, tm):
    t, d = x2.shape
    pairs = ssm_w // S5_PAIR_CH
    nch = tm // S5_CHUNK
    perm = _row_permutation(nch, S5_CHUNK)
    row = lambda w: pl.BlockSpec((tm, w), lambda i: (i, 0))
    out_shape = (
        jax.ShapeDtypeStruct((pairs, t // S5_CHUNK, S5_CHUNK * S5_PAIR_CH), BF16),
        jax.ShapeDtypeStruct((t, ret_w), F32), jax.ShapeDtypeStruct((t, ret_w), F32),
        jax.ShapeDtypeStruct((t, ret_w), BF16), jax.ShapeDtypeStruct((t, ret_w), F32),
        jax.ShapeDtypeStruct((t, sb_w), BF16), jax.ShapeDtypeStruct((t, sb_w), BF16),
        jax.ShapeDtypeStruct((t, sb_w), BF16),
    )
    out_specs = (pl.BlockSpec((pairs, nch, S5_CHUNK * S5_PAIR_CH), lambda i: (0, i, 0)),
                 row(ret_w), row(ret_w), row(ret_w), row(ret_w), row(sb_w), row(sb_w), row(sb_w))
    return pl.pallas_call(
        functools.partial(_proj_kernel, ssm_w=ssm_w, ret_w=ret_w, sb_w=sb_w),
        grid=(t // tm,),
        in_specs=[row(d), _layer_of(w_bf.shape, layer), _resident(perm.shape)],
        out_specs=out_specs,
        out_shape=out_shape,
        compiler_params=_cparams(("parallel",)),
        name="proj",
    )(x2, w_bf, perm)


def _s5_table_inputs(lam_re, lam_im, log_dt, b_re, b_im, c_re, c_im, d_skip):
    nl, g, p = lam_re.shape
    hch = b_re.shape[-1]
    pairs = g // 2
    lc = S5_CHUNK
    dt = jnp.exp(log_dt)[..., None]
    a_re, a_im = lam_re * dt, lam_im * dt

    def lam_pow(n):
        n = n[None, :, None, None]
        mag = jnp.exp(n * a_re[:, None])
        return mag * jnp.cos(n * a_im[:, None]), mag * jnp.sin(n * a_im[:, None])

    lb_re, lb_im = lam_pow(jnp.ones((1,), F32))
    lb_re, lb_im = lb_re[:, 0], lb_im[:, 0]
    den = lam_re * lam_re + lam_im * lam_im
    z_re = ((lb_re - 1.0) * lam_re + lb_im * lam_im) / den
    z_im = (lb_im * lam_re - (lb_re - 1.0) * lam_im) / den
    bb_re = z_re[..., None] * b_re - z_im[..., None] * b_im
    bb_im = z_re[..., None] * b_im + z_im[..., None] * b_re
    eye2 = jnp.eye(2, dtype=F32)

    def b_blocks(v):
        v = v.reshape(nl, pairs, 2, p, hch)
        return jnp.einsum('lkgph,gf->lkghfp', v, eye2).reshape(nl, pairs, 2 * hch, 2 * p)

    def c_blocks(v):
        v = v.reshape(nl, pairs, 2, hch, p)
        return jnp.einsum('lkghp,gf->lkghfp', v, eye2).reshape(nl, pairs, 2 * hch, 2 * p)

    pw_re, pw_im = lam_pow(jnp.arange(lc + 1, dtype=F32))
    pad = (-(lc + 1)) % SUBLANES

    def pair_rows(v):
        v = v.reshape(nl, lc + 1, pairs, 2 * p).transpose(0, 2, 1, 3)
        return jnp.pad(v, ((0, 0), (0, 0), (0, pad), (0, 0)))

    dd = d_skip.reshape(nl, pairs, 2 * hch)[..., None] * jnp.eye(2 * hch, lc * 2 * hch, dtype=F32)

    def pair_lanes(re, im):
        k = re.shape[1]
        v = jnp.stack([re.reshape(nl, k, pairs, 2 * p), im.reshape(nl, k, pairs, 2 * p)], axis=3)
        return v.reshape(nl, k, pairs * 4 * p)

    lam_pos = pair_lanes(*lam_pow(lc * jnp.arange(S5_SUPER, dtype=F32)))
    lam_sup = pair_lanes(*lam_pow(jnp.full((1,), float(lc * S5_SUPER), F32)))
    return (b_blocks(bb_re), b_blocks(bb_im), c_blocks(c_re), c_blocks(c_im), pair_rows(pw_re), pair_rows(pw_im),
            dd, lam_pos, lam_sup)


def _s5_table_kernel(bbr_ref, bbi_ref, ctr_ref, cti_ref, pwr_ref, pwi_ref, dd_ref, mt_ref, wt_ref, vt_ref):
    lc = S5_CHUNK
    bbr, bbi, ctr, cti = bbr_ref[...], bbi_ref[...], ctr_ref[...], cti_ref[...]
    ch = bbr.shape[0]
    pwr = [pwr_ref[l:l + 1, :] for l in range(lc + 1)]
    pwi = [pwi_ref[l:l + 1, :] for l in range(lc + 1)]
    clr = [ctr * pwr[l] - cti * pwi[l] for l in range(lc + 1)]
    cli = [ctr * pwi[l] + cti * pwr[l] for l in range(lc + 1)]

    def dot3(a, b):
        ah, al = _split_bf16(a)
        bh, bl = _split_bf16(b)
        return _dot_nt(ah, bh) + _dot_nt(al, bh) + _dot_nt(ah, bl)

    r0 = (dot3(bbr, jnp.concatenate(clr[:lc], axis=0)) - dot3(bbi, jnp.concatenate(cli[:lc], axis=0))
          + dd_ref[...])
    lane = lax.broadcasted_iota(jnp.int32, r0.shape, 1)
    for s in range(lc):
        blk = r0 if s == 0 else jnp.where(lane >= s * ch, pltpu.roll(r0, s * ch, axis=1), 0.0)
        mt_ref[s * ch:(s + 1) * ch, :] = blk.astype(BF16)
        l = lc - 1 - s
        wt_ref[s * ch:(s + 1) * ch, :] = jnp.concatenate(
            [bbr * pwr[l] - bbi * pwi[l], bbr * pwi[l] + bbi * pwr[l]], axis=1).astype(BF16)
    vt_ref[...] = jnp.concatenate([jnp.concatenate(clr[1:], axis=0), -jnp.concatenate(cli[1:], axis=0)],
                                  axis=1).astype(BF16)


def _s5_tables(bbr, bbi, ctr, cti, pwr, pwi, dd):
    nl, pairs, ch, sdim = bbr.shape
    kdim = S5_CHUNK * ch
    blk = lambda a: pl.BlockSpec((None, None) + a.shape[2:], lambda l, k: (l, k, 0, 0))
    out = lambda w: pl.BlockSpec((None, None, kdim, w), lambda l, k: (l, k, 0, 0))
    return pl.pallas_call(
        _s5_table_kernel,
        grid=(nl, pairs),
        in_specs=[blk(bbr), blk(bbi), blk(ctr), blk(cti), blk(pwr), blk(pwi), blk(dd)],
        out_specs=(out(kdim), out(2 * sdim), out(2 * sdim)),
        out_shape=(jax.ShapeDtypeStruct((nl, pairs, kdim, kdim), BF16),
                   jax.ShapeDtypeStruct((nl, pairs, kdim, 2 * sdim), BF16),
                   jax.ShapeDtypeStruct((nl, pairs, kdim, 2 * sdim), BF16)),
        compiler_params=_cparams(("parallel", "parallel")),
        name="s5_tables",
    )(bbr, bbi, ctr, cti, pwr, pwi, dd)


def _cmul(l, x):
    out = []
    for c0 in range(0, x.shape[1], 2 * LANES):
        lr, li = l[:, c0:c0 + LANES], l[:, c0 + LANES:c0 + 2 * LANES]
        xr, xi = x[:, c0:c0 + LANES], x[:, c0 + LANES:c0 + 2 * LANES]
        out += [lr * xr - li * xi, lr * xi + li * xr]
    return jnp.concatenate(out, axis=1)


def _s5_state_kernel(a_ref, wt_ref, lpos_ref, lsup_ref, perm_ref, permt_ref, xp_ref,
                     carry_ref, sin_ref, loc_ref, sup_ref, *, pairs, rows):
    @pl.when(pl.program_id(1) == 0)
    def _():
        carry_ref[...] = jnp.zeros_like(carry_ref)

    ns = rows // S5_SUPER
    blk = 2 * LANES
    perm = perm_ref[...]
    for k in range(pairs):
        ap = jnp.dot(perm, a_ref[k], preferred_element_type=F32).astype(BF16)
        sin_ref[:, k * blk:(k + 1) * blk] = jnp.dot(ap, wt_ref[k], preferred_element_type=F32)

    lam_chunk = lpos_ref[1:2, :]
    x = jnp.zeros((ns, sin_ref.shape[1]), F32)
    for j in range(S5_SUPER):
        loc_ref[j * ns:(j + 1) * ns, :] = x
        x = _cmul(lam_chunk, x) + sin_ref[j * ns:(j + 1) * ns, :]
    carry = carry_ref[...]
    lam_sup = lsup_ref[...]
    for m in range(ns):
        sup_ref[m:m + 1, :] = carry
        carry = _cmul(lam_sup, carry) + x[m:m + 1, :]
    carry_ref[...] = carry
    sup = sup_ref[...]
    for j in range(S5_SUPER):
        loc_ref[j * ns:(j + 1) * ns, :] += _cmul(lpos_ref[j:j + 1, :], sup)
    xp_ref[...] = jnp.dot(permt_ref[...], loc_ref[...].astype(BF16), preferred_element_type=F32).astype(BF16)


def _s5_states(a, wt, lam_pos, lam_sup, layer, batch, rows):
    pairs, n, kdim = a.shape
    per_b = n // batch // rows
    width = lam_pos.shape[2]
    ns = rows // S5_SUPER
    perm = _row_permutation(ns, S5_SUPER)
    return pl.pallas_call(
        functools.partial(_s5_state_kernel, pairs=pairs, rows=rows),
        grid=(batch, per_b),
        in_specs=[pl.BlockSpec((pairs, rows, kdim), lambda b, i: (0, b * per_b + i, 0)),
                  _layer_of(wt.shape, layer), _layer_of(lam_pos.shape, layer), _layer_of(lam_sup.shape, layer),
                  _resident(perm.shape), _resident(perm.shape)],
        out_specs=pl.BlockSpec((rows, width), lambda b, i: (b * per_b + i, 0)),
        out_shape=jax.ShapeDtypeStruct((n, width), BF16),
        scratch_shapes=[pltpu.VMEM((1, width), F32), pltpu.VMEM((rows, width), F32),
                        pltpu.VMEM((rows, width), F32), pltpu.VMEM((ns, width), F32)],
        compiler_params=_cparams(("arbitrary", "arbitrary")),
        name="s5_states",
    )(a, wt, lam_pos, lam_sup, perm, perm.T)


def _s5_out_kernel(a_ref, mt_ref, xp_ref, vt_ref, gw_ref, gb_ref, nw_ref, gm_ref, permt_ref, o_ref, *, pairs):
    rows = a_ref.shape[1]
    ch = S5_PAIR_CH
    sdim = vt_ref.shape[2]
    ys = [jnp.dot(a_ref[k], mt_ref[k], preferred_element_type=F32)
          + _dot_nt(xp_ref[:, k * sdim:(k + 1) * sdim], vt_ref[k]) for k in range(pairs)]
    outs = []
    for t in range(S5_CHUNK):
        y = jnp.concatenate([ys[k][:, t * ch:(t + 1) * ch] for k in range(pairs)], axis=1)
        g = y * (0.5 * (1.0 + jnp.tanh(math.sqrt(2.0 / math.pi) * (y + 0.044715 * (y * y * y)))))
        z = jnp.dot(g.astype(BF16), gw_ref[...], preferred_element_type=F32) + gb_ref[...]
        out = g * _sigmoid(z)
        hi, lo = _split_bf16(out * out)
        ms = (jnp.dot(hi, gm_ref[...], preferred_element_type=F32)
              + jnp.dot(lo, gm_ref[...], preferred_element_type=F32))
        outs.append((out * lax.rsqrt(ms + LN_EPS) * nw_ref[...]).astype(BF16))
    gch = permt_ref.shape[0] // S5_CHUNK
    for gi in range(rows // gch):
        stack = jnp.concatenate([outs[t][gi * gch:(gi + 1) * gch, :] for t in range(S5_CHUNK)], axis=0)
        o_ref[gi * gch * S5_CHUNK:(gi + 1) * gch * S5_CHUNK, :] = jnp.dot(
            permt_ref[...], stack, preferred_element_type=F32).astype(BF16)


def _s5_outputs(a, mt, xp, vt, glu_w, glu_b, norm_w, layer, rows):
    pairs, n, kdim = a.shape
    w = glu_w.shape[1]
    gch = min(ROW_TILE // S5_CHUNK, rows)
    groups = jnp.arange(w) // SSM_GROUP_CH
    gmean = jnp.where(groups[:, None] == groups[None, :], 1.0 / SSM_GROUP_CH, 0.0).astype(BF16)
    permt = _row_permutation(gch, S5_CHUNK).T
    return pl.pallas_call(
        functools.partial(_s5_out_kernel, pairs=pairs),
        grid=(n // rows,),
        in_specs=[pl.BlockSpec((pairs, rows, kdim), lambda i: (0, i, 0)), _layer_of(mt.shape, layer),
                  pl.BlockSpec((rows, xp.shape[1]), lambda i: (i, 0)), _layer_of(vt.shape, layer),
                  _layer_of(glu_w.shape, layer), _layer_of(glu_b.shape, layer), _layer_of(norm_w.shape, layer),
                  _resident(gmean.shape), _resident(permt.shape)],
        out_specs=pl.BlockSpec((rows * S5_CHUNK, w), lambda i: (i, 0)),
        out_shape=jax.ShapeDtypeStruct((n * S5_CHUNK, w), BF16),
        compiler_params=_cparams(("parallel",)),
        name="s5_outputs",
    )(a, mt, xp, vt, glu_w, glu_b, norm_w, gmean, permt)


def _retention_kernel(q_ref, k_ref, v_ref, g_ref, cos_ref, sin_ref, qdec_ref, kdec_ref, dmask_ref,
                      cdec_ref, gnw_ref, gnb_ref, gm_ref, bmask_ref, o_ref, state_ref, *, heads, chunk):
    width = q_ref.shape[1]
    c = chunk
    half_w = MXU_DIM
    halves = width // half_w
    heads_per_half = heads // halves
    dh = width // heads

    @pl.when(pl.program_id(1) == 0)
    def _():
        state_ref[...] = jnp.zeros_like(state_ref)

    lane = lax.broadcasted_iota(jnp.int32, (c, width), 1)
    first_half = (lane % dh) < (dh // 2)
    lane_h = lax.broadcasted_iota(jnp.int32, (c, half_w), 1) // dh

    def group_mean(a):
        hi, lo = _split_bf16(a)
        parts = []
        for hf in range(halves):
            sl = slice(hf * half_w, (hf + 1) * half_w)
            parts.append(jnp.dot(hi[:, sl], gm_ref[...], preferred_element_type=F32)
                         + jnp.dot(lo[:, sl], gm_ref[...], preferred_element_type=F32))
        return jnp.concatenate(parts, axis=1)

    for step in range(q_ref.shape[0] // c):
        rs = slice(step * c, (step + 1) * c)

        def rope(x):
            swapped = jnp.where(first_half, pltpu.roll(x, width - dh // 2, axis=1), pltpu.roll(x, dh // 2, axis=1))
            return x * cos_ref[rs, :] + swapped * sin_ref[rs, :]

        q = rope(q_ref[rs, :])
        k = rope(k_ref[rs, :]) * (dh ** -0.5)
        v = v_ref[rs, :]
        qb = q.astype(BF16)
        kb = k.astype(BF16)
        qd = (q * qdec_ref[...]).astype(BF16)
        kd = k * kdec_ref[...]

        outs = []
        for hf in range(halves):
            sl = slice(hf * half_w, (hf + 1) * half_w)
            qh, kh, vh = qb[:, sl], kb[:, sl], v[:, sl]
            scores, vstack = [], []
            for j in range(heads_per_half):
                sel = lane_h == j
                s = _dot_nt(jnp.where(sel, qh, jnp.zeros_like(qh)), kh)
                scores.append((s * dmask_ref[hf * heads_per_half + j]).astype(BF16))
                vstack.append(jnp.where(sel, vh, jnp.zeros_like(vh)))
            inner = jnp.dot(jnp.concatenate(scores, axis=1), jnp.concatenate(vstack, axis=0),
                            preferred_element_type=F32)
            st = state_ref[hf]
            cross = jnp.dot(qd[:, sl], st.astype(BF16), preferred_element_type=F32)
            outs.append(inner + cross)
            upd = jnp.dot(kd[:, sl].T.astype(BF16), vh, preferred_element_type=F32)
            state_ref[hf] = st * cdec_ref[:, sl] + upd * bmask_ref[...]
        o = jnp.concatenate(outs, axis=1)

        d = o - group_mean(o)
        var = group_mean(d * d)
        on = d * lax.rsqrt(var + LN_EPS) * gnw_ref[...] + gnb_ref[...]
        g = g_ref[rs, :]
        o_ref[rs, :] = (g * _sigmoid(g) * on).astype(BF16)


def _retention_tables(seq, heads):
    dh = RET_HEAD_DIM
    c = RET_CHUNK
    half = dh // 2
    pos = jnp.arange(seq, dtype=F32)
    inv = ROPE_BASE ** (-jnp.arange(half, dtype=F32) / half)
    ang = pos[:, None] * inv[None, :]
    cos_t = jnp.tile(jnp.concatenate([jnp.cos(ang), jnp.cos(ang)], axis=1), (1, heads))
    sin_t = jnp.tile(jnp.concatenate([-jnp.sin(ang), jnp.sin(ang)], axis=1), (1, heads))
    log_gamma = jnp.log(1.0 - 2.0 ** (-5.0 - jnp.arange(heads, dtype=F32)))
    idx = jnp.arange(c, dtype=F32)
    rel = idx[:, None] - idx[None, :]
    dmask = jnp.where(rel >= 0, jnp.exp(log_gamma[:, None, None] * jnp.maximum(rel, 0.0)), 0.0)
    per_lane = lambda a: jnp.repeat(a, dh, axis=-1)
    qdec = per_lane(jnp.exp(log_gamma[None, :] * (idx[:, None] + 1.0)))
    kdec = per_lane(jnp.exp(log_gamma[None, :] * (c - 1.0 - idx[:, None])))
    cdec = per_lane(jnp.exp(log_gamma * c)[None, :])
    hid = jnp.arange(MXU_DIM) // dh
    same = hid[:, None] == hid[None, :]
    return cos_t, sin_t, qdec, kdec, dmask, cdec, jnp.where(same, 1.0 / dh, 0.0).astype(BF16), same.astype(F32)


def _retention_mixer(rq, rk, rv, rg, tables, gn_w, gn_b, layer, batch, seq):
    t, width = rq.shape
    heads = width // RET_HEAD_DIM
    c = RET_CHUNK
    rows = c * min(RET_STEPS, seq // c)
    n = seq // rows
    cos_t, sin_t, qdec, kdec, dmask, cdec, gmean, bmask = tables
    row = pl.BlockSpec((rows, width), lambda b, i: (b * n + i, 0))
    tab = pl.BlockSpec((rows, width), lambda b, i: (i, 0))
    halves = width // MXU_DIM
    return pl.pallas_call(
        functools.partial(_retention_kernel, heads=heads, chunk=c),
        grid=(batch, n),
        in_specs=[row, row, row, row, tab, tab, _resident((c, width)), _resident((c, width)),
                  _resident((heads, c, c)), _resident((1, width)), _layer_of(gn_w.shape, layer),
                  _layer_of(gn_b.shape, layer), _resident((MXU_DIM, MXU_DIM)), _resident((MXU_DIM, MXU_DIM))],
        out_specs=row,
        out_shape=jax.ShapeDtypeStruct((t, width), BF16),
        scratch_shapes=[pltpu.VMEM((halves, MXU_DIM, MXU_DIM), F32)],
        compiler_params=_cparams(("arbitrary", "arbitrary")),
        name="retention",
    )(rq, rk, rv, rg, cos_t, sin_t, qdec, kdec, dmask, cdec, gn_w, gn_b, gmean, bmask)


def _sb_kernel(q_ref, k_ref, v_ref, nw_ref, tri_ref, gm_ref, o_ref, acc_ref, run_ref, *, heads):
    tq, width = q_ref.shape
    tk = tq
    dh = width // heads
    i = pl.program_id(1)
    q = q_ref[...] * (dh ** -0.5)
    lane_q = lax.broadcasted_iota(jnp.int32, (tq, width), 1) // dh
    lane_k = lax.broadcasted_iota(jnp.int32, (tk, width), 1) // dh
    qms = [jnp.where(lane_q == h, q, jnp.zeros_like(q)) for h in range(heads)]
    causal = lax.broadcasted_iota(jnp.int32, (tq, tk), 1) < lax.broadcasted_iota(jnp.int32, (tq, tk), 0)

    def tile(j, masked):
        start = pl.multiple_of(j * tk, tk)
        kb = k_ref[pl.ds(start, tk), :]
        vb = v_ref[pl.ds(start, tk), :]
        weights, vstack, low = [], [], None
        for h in range(heads):
            z = _dot_nt(qms[h], kb)
            sp = jnp.maximum(z, 0.0) + jnp.log(1.0 + jnp.exp2(jnp.abs(z) * (-LOG2_E)))
            sp_sum = jnp.where(causal, sp, 0.0) if masked else sp
            hi, lo = _split_bf16(sp_sum)
            later = jnp.dot(jnp.concatenate([hi, lo], axis=1), tri_ref[...], preferred_element_type=F32)
            run = run_ref[h]
            a = jnp.exp2((z - sp - later - run) * LOG2_E)
            if masked:
                a = jnp.where(causal, a, 0.0)
            weights.append(a.astype(BF16))
            vstack.append(jnp.where(lane_k == h, vb, jnp.zeros_like(vb)))
            run = run + jnp.sum(sp_sum, axis=1, keepdims=True)
            run_ref[h] = run
            m = jnp.min(run)
            low = m if low is None else jnp.minimum(low, m)
        acc_ref[...] += jnp.dot(jnp.concatenate(weights, axis=1), jnp.concatenate(vstack, axis=0),
                                preferred_element_type=F32)
        return low

    acc_ref[...] = jnp.zeros_like(acc_ref)
    run_ref[...] = jnp.zeros_like(run_ref)
    low = tile(i, True)

    def cond(carry):
        j, low = carry
        return jnp.logical_and(j >= 0, low < SB_EXIT_LOG)

    def body(carry):
        j, _ = carry
        return j - 1, tile(j, False)

    lax.while_loop(cond, body, (i - 1, low))
    o = acc_ref[...]
    hi, lo = _split_bf16(o * o)
    ms = (jnp.dot(hi, gm_ref[...], preferred_element_type=F32) + jnp.dot(lo, gm_ref[...], preferred_element_type=F32))
    o_ref[...] = (o * lax.rsqrt(ms + LN_EPS) * nw_ref[...]).astype(BF16)


def _sb_mixer(sq, sk, sv, norm_w, layer, batch, seq):
    t, width = sq.shape
    heads = width // SB_HEAD_DIM
    tq = min(SB_TILE, seq)
    nq = seq // tq
    idx = jnp.arange(tq)
    tri = (idx[:, None] > idx[None, :]).astype(BF16)
    tri2 = jnp.concatenate([tri, tri], axis=0)
    hid = jnp.arange(width) // SB_HEAD_DIM
    gmean = jnp.where(hid[:, None] == hid[None, :], 1.0 / SB_HEAD_DIM, 0.0).astype(BF16)
    whole = pl.BlockSpec((None, seq, width), lambda b, i: (b, 0, 0))
    row = pl.BlockSpec((tq, width), lambda b, i: (b * nq + i, 0))
    return pl.pallas_call(
        functools.partial(_sb_kernel, heads=heads),
        grid=(batch, nq),
        in_specs=[row, whole, whole, _layer_of(norm_w.shape, layer), _resident((2 * tq, tq)),
                  _resident((width, width))],
        out_specs=row,
        out_shape=jax.ShapeDtypeStruct((t, width), BF16),
        scratch_shapes=[pltpu.VMEM((tq, width), F32), pltpu.VMEM((heads, tq, 1), F32)],
        compiler_params=_cparams(("parallel", "arbitrary")),
        name="stick_breaking",
    )(sq, sk.reshape(batch, seq, width), sv.reshape(batch, seq, width), norm_w, tri2, gmean)


def _mix_kernel(x_ref, ya_ref, yb_ref, yc_ref, w_ref, lw_ref, lb_ref, o_ref, *, alpha):
    wa = ya_ref.shape[1]
    wb = yb_ref.shape[1]
    mix = (jnp.dot(ya_ref[...], w_ref[0:wa, :], preferred_element_type=F32)
           + jnp.dot(yb_ref[...], w_ref[wa:wa + wb, :], preferred_element_type=F32)
           + jnp.dot(yc_ref[...], w_ref[wa + wb:, :], preferred_element_type=F32))
    o_ref[...] = _layer_norm_rows(alpha * x_ref[...] + mix, lw_ref[...], lb_ref[...])


def _mix(x2, ya, yb, yc, w_bf, ln_w, ln_b, layer, alpha, tm):
    t, d = x2.shape
    row = lambda w: pl.BlockSpec((tm, w), lambda i: (i, 0))
    return pl.pallas_call(
        functools.partial(_mix_kernel, alpha=alpha),
        grid=(t // tm,),
        in_specs=[row(d), row(ya.shape[1]), row(yb.shape[1]), row(yc.shape[1]), _layer_of(w_bf.shape, layer),
                  _layer_of(ln_w.shape, layer), _layer_of(ln_b.shape, layer)],
        out_specs=row(d),
        out_shape=jax.ShapeDtypeStruct((t, d), F32),
        compiler_params=_cparams(("parallel",)),
        name="mix_ln",
    )(x2, ya, yb, yc, w_bf, ln_w, ln_b)


def _swiglu_acc(xb, wg_ref, wu_ref, wd_ref, acc_ref, chunk, scale=None):
    ff = wg_ref.shape[-1]
    for c0 in range(0, ff, chunk):
        g = jnp.dot(xb, wg_ref[:, c0:c0 + chunk], preferred_element_type=F32)
        u = jnp.dot(xb, wu_ref[:, c0:c0 + chunk], preferred_element_type=F32)
        h = g * _sigmoid(g) * u
        if scale is not None:
            h = h * scale
        acc_ref[...] += jnp.dot(h.astype(BF16), wd_ref[c0:c0 + chunk, :], preferred_element_type=F32)


def _ffn_kernel(x_ref, wg_ref, wu_ref, wd_ref, lw_ref, lb_ref, o_ref, acc_ref, *, alpha):
    x = x_ref[...]
    acc_ref[...] = jnp.zeros_like(acc_ref)
    _swiglu_acc(x.astype(BF16), wg_ref, wu_ref, wd_ref, acc_ref, MXU_DIM)
    o_ref[...] = _layer_norm_rows(alpha * x + acc_ref[...], lw_ref[...], lb_ref[...])


def _pad_to(a, axis, mult):
    size = a.shape[axis]
    pad = (-size) % mult
    if pad == 0:
        return a
    widths = [(0, 0)] * a.ndim
    widths[axis] = (0, pad)
    return jnp.pad(a, widths)


def _ffn(x2, wg, wu, wd, ln_w, ln_b, li, layer, alpha, tm):
    t, d = x2.shape
    row = pl.BlockSpec((tm, d), lambda i: (i, 0))
    return pl.pallas_call(
        functools.partial(_ffn_kernel, alpha=alpha),
        grid=(t // tm,),
        in_specs=[row, _layer_of(wg.shape, li), _layer_of(wu.shape, li), _layer_of(wd.shape, li),
                  _layer_of(ln_w.shape, layer), _layer_of(ln_b.shape, layer)],
        out_specs=row,
        out_shape=jax.ShapeDtypeStruct((t, d), F32),
        scratch_shapes=[pltpu.VMEM((tm, d), F32)],
        compiler_params=_cparams(("parallel",)),
        name="ffn_ln",
    )(x2, wg, wu, wd, ln_w, ln_b)


def _router_kernel(x_ref, wh_ref, wl_ref, g_ref, *, n_experts):
    xh, xl = _split_bf16(x_ref[...])
    logits = (jnp.dot(xh, wh_ref[...], preferred_element_type=F32)
              + jnp.dot(xl, wh_ref[...], preferred_element_type=F32)
              + jnp.dot(xh, wl_ref[...], preferred_element_type=F32))
    lane = lax.broadcasted_iota(jnp.int32, logits.shape, 1).astype(F32)
    neg = -1e30
    logits = jnp.where(lane < n_experts, logits, neg)
    m1 = jnp.max(logits, axis=1, keepdims=True)
    i1 = jnp.min(jnp.where(logits == m1, lane, float(LANES)), axis=1, keepdims=True)
    rest = jnp.where(lane == i1, neg, logits)
    m2 = jnp.max(rest, axis=1, keepdims=True)
    i2 = jnp.min(jnp.where(rest == m2, lane, float(LANES)), axis=1, keepdims=True)
    e2 = jnp.exp(m2 - m1)
    w1 = 1.0 / (1.0 + e2)
    g_ref[...] = jnp.where(lane == i1, w1, 0.0) + jnp.where(lane == i2, e2 * w1, 0.0)


def _router(x2, wh, wl, li, n_experts, tm):
    t, d = x2.shape
    row = pl.BlockSpec((tm, d), lambda i: (i, 0))
    return pl.pallas_call(
        functools.partial(_router_kernel, n_experts=n_experts),
        grid=(t // tm,),
        in_specs=[row, _layer_of(wh.shape, li), _layer_of(wl.shape, li)],
        out_specs=pl.BlockSpec((tm, LANES), lambda i: (i, 0)),
        out_shape=jax.ShapeDtypeStruct((t, LANES), F32),
        compiler_params=_cparams(("parallel",)),
        name="router",
    )(x2, wh, wl)


def _moe_kernel(x_ref, gates_ref, wg_ref, wu_ref, wd_ref, lw_ref, lb_ref, o_ref, acc_ref, *, alpha):
    e = pl.program_id(1)

    @pl.when(e == 0)
    def _():
        acc_ref[...] = jnp.zeros_like(acc_ref)

    gates = gates_ref[...]
    lane = lax.broadcasted_iota(jnp.int32, gates.shape, 1)
    gate = jnp.sum(jnp.where(lane == e, gates, 0.0), axis=1, keepdims=True)
    x = x_ref[...]
    _swiglu_acc(x.astype(BF16), wg_ref, wu_ref, wd_ref, acc_ref, MXU_DIM, scale=gate)

    @pl.when(e == pl.num_programs(1) - 1)
    def _():
        o_ref[...] = _layer_norm_rows(alpha * x + acc_ref[...], lw_ref[...], lb_ref[...])


def _moe(x2, gates, wg, wu, wd, ln_w, ln_b, li, layer, alpha, tm):
    t, d = x2.shape
    n_experts, ffp = wg.shape[1], wg.shape[3]
    row = pl.BlockSpec((tm, d), lambda i, e: (i, 0))
    return pl.pallas_call(
        functools.partial(_moe_kernel, alpha=alpha),
        grid=(t // tm, n_experts),
        in_specs=[row, pl.BlockSpec((tm, LANES), lambda i, e: (i, 0)),
                  pl.BlockSpec((None, None, d, ffp), lambda i, e: (li, e, 0, 0)),
                  pl.BlockSpec((None, None, d, ffp), lambda i, e: (li, e, 0, 0)),
                  pl.BlockSpec((None, None, ffp, d), lambda i, e: (li, e, 0, 0)),
                  _layer_of(ln_w.shape, layer), _layer_of(ln_b.shape, layer)],
        out_specs=row,
        out_shape=jax.ShapeDtypeStruct((t, d), F32),
        scratch_shapes=[pltpu.VMEM((tm, d), F32)],
        compiler_params=_cparams(("parallel", "arbitrary")),
        name="moe_ln",
    )(x2, gates, wg, wu, wd, ln_w, ln_b)


def kernel(x, w_in, ssm_lam_re, ssm_lam_im, ssm_log_dt, ssm_b_re, ssm_b_im, ssm_c_re, ssm_c_im, ssm_d, ssm_glu_w, ssm_glu_b, ssm_norm_w, ret_gn_w, ret_gn_b, sb_norm_w, w_out, ln_mix_w, ln_mix_b, ffn_w_gate, ffn_w_up, ffn_w_down, moe_router, moe_w_gate, moe_w_up, moe_w_down, ln_ffn_w, ln_ffn_b):
    batch, seq, d = x.shape
    depth = w_in.shape[0]
    alpha = (2.0 * depth) ** 0.25
    ssm_w = ssm_d.shape[1]
    ret_w = ret_gn_w.shape[1]
    sb_w = sb_norm_w.shape[1]
    t = batch * seq
    tm = min(ROW_TILE, t)
    vec = lambda a: a[:, None, :]

    w_in_bf, w_out_bf = w_in.astype(BF16), w_out.astype(BF16)
    glu_w_bf = ssm_glu_w.astype(BF16)
    ffn_g = _pad_to(ffn_w_gate, 2, MXU_DIM).astype(BF16)
    ffn_u = _pad_to(ffn_w_up, 2, MXU_DIM).astype(BF16)
    ffn_d = _pad_to(ffn_w_down, 1, MXU_DIM).astype(BF16)
    moe_g = _pad_to(moe_w_gate, 3, MXU_DIM).astype(BF16)
    moe_u = _pad_to(moe_w_up, 3, MXU_DIM).astype(BF16)
    moe_d = _pad_to(moe_w_down, 2, MXU_DIM).astype(BF16)
    n_experts = moe_router.shape[2]
    router_h, router_l = _split_bf16(_pad_to(moe_router, 2, LANES))
    table_in = _s5_table_inputs(ssm_lam_re, ssm_lam_im, ssm_log_dt, ssm_b_re, ssm_b_im, ssm_c_re, ssm_c_im, ssm_d)
    s5_mt, s5_wt, s5_vt = _s5_tables(*table_in[:7])
    lam_pos, lam_sup = table_in[7:]
    ret_tables = _retention_tables(seq, ret_w // RET_HEAD_DIM)
    s5_rows = min(S5_ROWS, seq // S5_CHUNK)

    h = x.reshape(t, d)
    for layer in range(depth):
        a, rq, rk, rv, rg, sq, sk, sv = _proj(h, w_in_bf, layer, ssm_w, ret_w, sb_w, tm)
        xp = _s5_states(a, s5_wt, lam_pos, lam_sup, layer, batch, s5_rows)
        y_ssm = _s5_outputs(a, s5_mt, xp, s5_vt, glu_w_bf, vec(ssm_glu_b), vec(ssm_norm_w), layer, s5_rows)
        y_ret = _retention_mixer(rq, rk, rv, rg, ret_tables, vec(ret_gn_w), vec(ret_gn_b), layer, batch, seq)
        y_sb = _sb_mixer(sq, sk, sv, vec(sb_norm_w), layer, batch, seq)
        h = _mix(h, y_ssm, y_ret, y_sb, w_out_bf, vec(ln_mix_w), vec(ln_mix_b), layer, alpha, tm)
        li = layer // 2
        if layer % 2 == 0:
            h = _ffn(h, ffn_g, ffn_u, ffn_d, vec(ln_ffn_w), vec(ln_ffn_b), li, layer, alpha, tm)
        else:
            gates = _router(h, router_h, router_l, li, n_experts, tm)
            h = _moe(h, gates, moe_g, moe_u, moe_d, vec(ln_ffn_w), vec(ln_ffn_b), li, layer, alpha, tm)
    return h.reshape(batch, seq, d)
```

```python
import functools
import math

import jax
import jax.numpy as jnp
from jax import lax
from jax.experimental import pallas as pl
from jax.experimental.pallas import tpu as pltpu

F32 = jnp.float32
BF16 = jnp.bfloat16

SSM_GROUP_CH = 16
SSM_STATE = 64
RET_HEAD_DIM = 64
SB_HEAD_DIM = 64
RET_CHUNK = 128
ROPE_BASE = 10000.0
TOP_K = 2
LN_EPS = 1e-5
LOG2_E = math.log2(math.e)

LANES = 128
SUBLANES = 8
MXU_DIM = 256
VMEM_LIMIT_BYTES = 56 * 1024 * 1024

ROW_TILE = 512
S5_CHUNK = 16
S5_PAIR_CH = 2 * SSM_GROUP_CH
S5_SUPER = 16
S5_ROWS = 256
RET_STEPS = 2
SB_TILE = 256
SB_EXIT_LOG = 110.0
DMA_UNROLL = 8


def _cparams(sem, vmem=VMEM_LIMIT_BYTES):
    return pltpu.CompilerParams(dimension_semantics=sem, vmem_limit_bytes=vmem)


def _resident(shape):
    nd = len(shape)
    return pl.BlockSpec(shape, lambda *_: (0,) * nd)


def _layer_of(shape, layer):
    nd = len(shape)
    return pl.BlockSpec((None,) + tuple(shape[1:]), lambda *_: (layer,) + (0,) * (nd - 1))


def _layer_norm_rows(h, w, b):
    mu = jnp.mean(h, axis=-1, keepdims=True)
    d = h - mu
    var = jnp.mean(d * d, axis=-1, keepdims=True)
    return d * lax.rsqrt(var + LN_EPS) * w + b


def _split_bf16(v):
    hi = v.astype(BF16)
    lo = (v - hi.astype(F32)).astype(BF16)
    return hi, lo


def _dot_nt(a, b):
    return lax.dot_general(a, b, (((1,), (1,)), ((), ())), preferred_element_type=F32)


def _sigmoid(v):
    return 1.0 / (1.0 + jnp.exp(-v))


def _row_permutation(groups, per_group):
    n = groups * per_group
    src = jnp.arange(n)
    dst = (src % per_group) * groups + src // per_group
    return (dst[:, None] == jnp.arange(n)[None, :]).astype(BF16).T


def _proj_kernel(x_ref, w_ref, perm_ref, a_ref, rq_ref, rk_ref, rv_ref, rg_ref,
                 sq_ref, sk_ref, sv_ref, *, ssm_w, ret_w, sb_w):
    xb = x_ref[...].astype(BF16)
    col = [0]

    def mm(width):
        c0 = col[0]
        col[0] = c0 + width
        return jnp.dot(xb, w_ref[:, c0:c0 + width], preferred_element_type=F32)

    u = mm(ssm_w).astype(BF16)
    up = jnp.dot(perm_ref[...], u, preferred_element_type=F32).astype(BF16)
    nch = u.shape[0] // S5_CHUNK
    for k in range(ssm_w // S5_PAIR_CH):
        a_ref[k] = jnp.concatenate(
            [up[s * nch:(s + 1) * nch, k * S5_PAIR_CH:(k + 1) * S5_PAIR_CH] for s in range(S5_CHUNK)], axis=1)
    rq_ref[...] = mm(ret_w)
    rk_ref[...] = mm(ret_w)
    rv_ref[...] = mm(ret_w).astype(BF16)
    rg_ref[...] = mm(ret_w)
    sq_ref[...] = mm(sb_w).astype(BF16)
    sk_ref[...] = mm(sb_w).astype(BF16)
    sv_ref[...] = mm(sb_w).astype(BF16)


def _proj(x2, w_bf, layer, ssm_w, ret_w, sb_w, tm):
    t, d = x2.shape
    pairs = ssm_w // S5_PAIR_CH
    nch = tm // S5_CHUNK
    perm = _row_permutation(nch, S5_CHUNK)
    row = lambda w: pl.BlockSpec((tm, w), lambda i: (i, 0))
    out_shape = (
        jax.ShapeDtypeStruct((pairs, t // S5_CHUNK, S5_CHUNK * S5_PAIR_CH), BF16),
        jax.ShapeDtypeStruct((t, ret_w), F32), jax.ShapeDtypeStruct((t, ret_w), F32),
        jax.ShapeDtypeStruct((t, ret_w), BF16), jax.ShapeDtypeStruct((t, ret_w), F32),
        jax.ShapeDtypeStruct((t, sb_w), BF16), jax.ShapeDtypeStruct((t, sb_w), BF16),
        jax.ShapeDtypeStruct((t, sb_w), BF16),
    )
    out_specs = (pl.BlockSpec((pairs, nch, S5_CHUNK * S5_PAIR_CH), lambda i: (0, i, 0)),
                 row(ret_w), row(ret_w), row(ret_w), row(ret_w), row(sb_w), row(sb_w), row(sb_w))
    return pl.pallas_call(
        functools.partial(_proj_kernel, ssm_w=ssm_w, ret_w=ret_w, sb_w=sb_w),
        grid=(t // tm,),
        in_specs=[row(d), _layer_of(w_bf.shape, layer), _resident(perm.shape)],
        out_specs=out_specs,
        out_shape=out_shape,
        compiler_params=_cparams(("parallel",)),
        name="proj",
    )(x2, w_bf, perm)


def _s5_table_inputs(lam_re, lam_im, log_dt, b_re, b_im, c_re, c_im, d_skip):
    nl, g, p = lam_re.shape
    hch = b_re.shape[-1]
    pairs = g // 2
    lc = S5_CHUNK
    dt = jnp.exp(log_dt)[..., None]
    a_re, a_im = lam_re * dt, lam_im * dt

    def lam_pow(n):
        n = n[None, :, None, None]
        mag = jnp.exp(n * a_re[:, None])
        return mag * jnp.cos(n * a_im[:, None]), mag * jnp.sin(n * a_im[:, None])

    lb_re, lb_im = lam_pow(jnp.ones((1,), F32))
    lb_re, lb_im = lb_re[:, 0], lb_im[:, 0]
    den = lam_re * lam_re + lam_im * lam_im
    z_re = ((lb_re - 1.0) * lam_re + lb_im * lam_im) / den
    z_im = (lb_im * lam_re - (lb_re - 1.0) * lam_im) / den
    bb_re = z_re[..., None] * b_re - z_im[..., None] * b_im
    bb_im = z_re[..., None] * b_im + z_im[..., None] * b_re
    eye2 = jnp.eye(2, dtype=F32)

    def b_blocks(v):
        v = v.reshape(nl, pairs, 2, p, hch)
        return jnp.einsum('lkgph,gf->lkghfp', v, eye2).reshape(nl, pairs, 2 * hch, 2 * p)

    def c_blocks(v):
        v = v.reshape(nl, pairs, 2, hch, p)
        return jnp.einsum('lkghp,gf->lkghfp', v, eye2).reshape(nl, pairs, 2 * hch, 2 * p)

    pw_re, pw_im = lam_pow(jnp.arange(lc + 1, dtype=F32))
    pad = (-(lc + 1)) % SUBLANES

    def pair_rows(v):
        v = v.reshape(nl, lc + 1, pairs, 2 * p).transpose(0, 2, 1, 3)
        return jnp.pad(v, ((0, 0), (0, 0), (0, pad), (0, 0)))

    dd = d_skip.reshape(nl, pairs, 2 * hch)[..., None] * jnp.eye(2 * hch, lc * 2 * hch, dtype=F32)

    def pair_lanes(re, im):
        k = re.shape[1]
        v = jnp.stack([re.reshape(nl, k, pairs, 2 * p), im.reshape(nl, k, pairs, 2 * p)], axis=3)
        return v.reshape(nl, k, pairs * 4 * p)

    lam_pos = pair_lanes(*lam_pow(lc * jnp.arange(S5_SUPER, dtype=F32)))
    lam_sup = pair_lanes(*lam_pow(jnp.full((1,), float(lc * S5_SUPER), F32)))
    return (b_blocks(bb_re), b_blocks(bb_im), c_blocks(c_re), c_blocks(c_im), pair_rows(pw_re), pair_rows(pw_im),
            dd, lam_pos, lam_sup)


def _s5_table_kernel(bbr_ref, bbi_ref, ctr_ref, cti_ref, pwr_ref, pwi_ref, dd_ref, mt_ref, wt_ref, vt_ref):
    lc = S5_CHUNK
    bbr, bbi, ctr, cti = bbr_ref[...], bbi_ref[...], ctr_ref[...], cti_ref[...]
    ch = bbr.shape[0]
    pwr = [pwr_ref[l:l + 1, :] for l in range(lc + 1)]
    pwi = [pwi_ref[l:l + 1, :] for l in range(lc + 1)]
    clr = [ctr * pwr[l] - cti * pwi[l] for l in range(lc + 1)]
    cli = [ctr * pwi[l] + cti * pwr[l] for l in range(lc + 1)]

    def dot3(a, b):
        ah, al = _split_bf16(a)
        bh, bl = _split_bf16(b)
        return _dot_nt(ah, bh) + _dot_nt(al, bh) + _dot_nt(ah, bl)

    r0 = (dot3(bbr, jnp.concatenate(clr[:lc], axis=0)) - dot3(bbi, jnp.concatenate(cli[:lc], axis=0))
          + dd_ref[...])
    lane = lax.broadcasted_iota(jnp.int32, r0.shape, 1)
    for s in range(lc):
        blk = r0 if s == 0 else jnp.where(lane >= s * ch, pltpu.roll(r0, s * ch, axis=1), 0.0)
        mt_ref[s * ch:(s + 1) * ch, :] = blk.astype(BF16)
        l = lc - 1 - s
        wt_ref[s * ch:(s + 1) * ch, :] = jnp.concatenate(
            [bbr * pwr[l] - bbi * pwi[l], bbr * pwi[l] + bbi * pwr[l]], axis=1).astype(BF16)
    vt_ref[...] = jnp.concatenate([jnp.concatenate(clr[1:], axis=0), -jnp.concatenate(cli[1:], axis=0)],
                                  axis=1).astype(BF16)


def _s5_tables(bbr, bbi, ctr, cti, pwr, pwi, dd):
    nl, pairs, ch, sdim = bbr.shape
    kdim = S5_CHUNK * ch
    blk = lambda a: pl.BlockSpec((None, None) + a.shape[2:], lambda l, k: (l, k, 0, 0))
    out = lambda w: pl.BlockSpec((None, None, kdim, w), lambda l, k: (l, k, 0, 0))
    return pl.pallas_call(
        _s5_table_kernel,
        grid=(nl, pairs),
        in_specs=[blk(bbr), blk(bbi), blk(ctr), blk(cti), blk(pwr), blk(pwi), blk(dd)],
        out_specs=(out(kdim), out(2 * sdim), out(2 * sdim)),
        out_shape=(jax.ShapeDtypeStruct((nl, pairs, kdim, kdim), BF16),
                   jax.ShapeDtypeStruct((nl, pairs, kdim, 2 * sdim), BF16),
                   jax.ShapeDtypeStruct((nl, pairs, kdim, 2 * sdim), BF16)),
        compiler_params=_cparams(("parallel", "parallel")),
        name="s5_tables",
    )(bbr, bbi, ctr, cti, pwr, pwi, dd)


def _cmul(l, x):
    out = []
    for c0 in range(0, x.shape[1], 2 * LANES):
        lr, li = l[:, c0:c0 + LANES], l[:, c0 + LANES:c0 + 2 * LANES]
        xr, xi = x[:, c0:c0 + LANES], x[:, c0 + LANES:c0 + 2 * LANES]
        out += [lr * xr - li * xi, lr * xi + li * xr]
    return jnp.concatenate(out, axis=1)


def _s5_state_kernel(a_ref, wt_ref, lpos_ref, lsup_ref, perm_ref, permt_ref, xp_ref,
                     carry_ref, sin_ref, loc_ref, sup_ref, *, pairs, rows):
    @pl.when(pl.program_id(1) == 0)
    def _():
        carry_ref[...] = jnp.zeros_like(carry_ref)

    ns = rows // S5_SUPER
    blk = 2 * LANES
    perm = perm_ref[...]
    for k in range(pairs):
        ap = jnp.dot(perm, a_ref[k], preferred_element_type=F32).astype(BF16)
        sin_ref[:, k * blk:(k + 1) * blk] = jnp.dot(ap, wt_ref[k], preferred_element_type=F32)

    lam_chunk = lpos_ref[1:2, :]
    x = jnp.zeros((ns, sin_ref.shape[1]), F32)
    for j in range(S5_SUPER):
        loc_ref[j * ns:(j + 1) * ns, :] = x
        x = _cmul(lam_chunk, x) + sin_ref[j * ns:(j + 1) * ns, :]
    carry = carry_ref[...]
    lam_sup = lsup_ref[...]
    for m in range(ns):
        sup_ref[m:m + 1, :] = carry
        carry = _cmul(lam_sup, carry) + x[m:m + 1, :]
    carry_ref[...] = carry
    sup = sup_ref[...]
    for j in range(S5_SUPER):
        loc_ref[j * ns:(j + 1) * ns, :] += _cmul(lpos_ref[j:j + 1, :], sup)
    xp_ref[...] = jnp.dot(permt_ref[...], loc_ref[...].astype(BF16), preferred_element_type=F32).astype(BF16)


def _s5_states(a, wt, lam_pos, lam_sup, layer, batch, rows):
    pairs, n, kdim = a.shape
    per_b = n // batch // rows
    width = lam_pos.shape[2]
    ns = rows // S5_SUPER
    perm = _row_permutation(ns, S5_SUPER)
    return pl.pallas_call(
        functools.partial(_s5_state_kernel, pairs=pairs, rows=rows),
        grid=(batch, per_b),
        in_specs=[pl.BlockSpec((pairs, rows, kdim), lambda b, i: (0, b * per_b + i, 0)),
                  _layer_of(wt.shape, layer), _layer_of(lam_pos.shape, layer), _layer_of(lam_sup.shape, layer),
                  _resident(perm.shape), _resident(perm.shape)],
        out_specs=pl.BlockSpec((rows, width), lambda b, i: (b * per_b + i, 0)),
        out_shape=jax.ShapeDtypeStruct((n, width), BF16),
        scratch_shapes=[pltpu.VMEM((1, width), F32), pltpu.VMEM((rows, width), F32),
                        pltpu.VMEM((rows, width), F32), pltpu.VMEM((ns, width), F32)],
        compiler_params=_cparams(("arbitrary", "arbitrary")),
        name="s5_states",
    )(a, wt, lam_pos, lam_sup, perm, perm.T)


def _s5_out_kernel(a_ref, mt_ref, xp_ref, vt_ref, gw_ref, gb_ref, nw_ref, gm_ref, permt_ref, o_ref, *, pairs):
    rows = a_ref.shape[1]
    ch = S5_PAIR_CH
    sdim = vt_ref.shape[2]
    ys = [jnp.dot(a_ref[k], mt_ref[k], preferred_element_type=F32)
          + _dot_nt(xp_ref[:, k * sdim:(k + 1) * sdim], vt_ref[k]) for k in range(pairs)]
    outs = []
    for t in range(S5_CHUNK):
        y = jnp.concatenate([ys[k][:, t * ch:(t + 1) * ch] for k in range(pairs)], axis=1)
        g = y * (0.5 * (1.0 + jnp.tanh(math.sqrt(2.0 / math.pi) * (y + 0.044715 * (y * y * y)))))
        z = jnp.dot(g.astype(BF16), gw_ref[...], preferred_element_type=F32) + gb_ref[...]
        out = g * _sigmoid(z)
        hi, lo = _split_bf16(out * out)
        ms = (jnp.dot(hi, gm_ref[...], preferred_element_type=F32)
              + jnp.dot(lo, gm_ref[...], preferred_element_type=F32))
        outs.append((out * lax.rsqrt(ms + LN_EPS) * nw_ref[...]).astype(BF16))
    gch = permt_ref.shape[0] // S5_CHUNK
    for gi in range(rows // gch):
        stack = jnp.concatenate([outs[t][gi * gch:(gi + 1) * gch, :] for t in range(S5_CHUNK)], axis=0)
        o_ref[gi * gch * S5_CHUNK:(gi + 1) * gch * S5_CHUNK, :] = jnp.dot(
            permt_ref[...], stack, preferred_element_type=F32).astype(BF16)


def _s5_outputs(a, mt, xp, vt, glu_w, glu_b, norm_w, layer, rows):
    pairs, n, kdim = a.shape
    w = glu_w.shape[1]
    gch = min(ROW_TILE // S5_CHUNK, rows)
    groups = jnp.arange(w) // SSM_GROUP_CH
    gmean = jnp.where(groups[:, None] == groups[None, :], 1.0 / SSM_GROUP_CH, 0.0).astype(BF16)
    permt = _row_permutation(gch, S5_CHUNK).T
    return pl.pallas_call(
        functools.partial(_s5_out_kernel, pairs=pairs),
        grid=(n // rows,),
        in_specs=[pl.BlockSpec((pairs, rows, kdim), lambda i: (0, i, 0)), _layer_of(mt.shape, layer),
                  pl.BlockSpec((rows, xp.shape[1]), lambda i: (i, 0)), _layer_of(vt.shape, layer),
                  _layer_of(glu_w.shape, layer), _layer_of(glu_b.shape, layer), _layer_of(norm_w.shape, layer),
                  _resident(gmean.shape), _resident(permt.shape)],
        out_specs=pl.BlockSpec((rows * S5_CHUNK, w), lambda i: (i, 0)),
        out_shape=jax.ShapeDtypeStruct((n * S5_CHUNK, w), BF16),
        compiler_params=_cparams(("parallel",)),
        name="s5_outputs",
    )(a, mt, xp, vt, glu_w, glu_b, norm_w, gmean, permt)


def _retention_kernel(q_ref, k_ref, v_ref, g_ref, cos_ref, sin_ref, qdec_ref, kdec_ref, dmask_ref,
                      cdec_ref, gnw_ref, gnb_ref, gm_ref, bmask_ref, o_ref, state_ref, *, heads, chunk):
    width = q_ref.shape[1]
    c = chunk
    half_w = MXU_DIM
    halves = width // half_w
    heads_per_half = heads // halves
    dh = width // heads

    @pl.when(pl.program_id(1) == 0)
    def _():
        state_ref[...] = jnp.zeros_like(state_ref)

    lane = lax.broadcasted_iota(jnp.int32, (c, width), 1)
    first_half = (lane % dh) < (dh // 2)
    lane_h = lax.broadcasted_iota(jnp.int32, (c, half_w), 1) // dh

    def group_mean(a):
        hi, lo = _split_bf16(a)
        parts = []
        for hf in range(halves):
            sl = slice(hf * half_w, (hf + 1) * half_w)
            parts.append(jnp.dot(hi[:, sl], gm_ref[...], preferred_element_type=F32)
                         + jnp.dot(lo[:, sl], gm_ref[...], preferred_element_type=F32))
        return jnp.concatenate(parts, axis=1)

    for step in range(q_ref.shape[0] // c):
        rs = slice(step * c, (step + 1) * c)

        def rope(x):
            swapped = jnp.where(first_half, pltpu.roll(x, width - dh // 2, axis=1), pltpu.roll(x, dh // 2, axis=1))
            return x * cos_ref[rs, :] + swapped * sin_ref[rs, :]

        q = rope(q_ref[rs, :])
        k = rope(k_ref[rs, :]) * (dh ** -0.5)
        v = v_ref[rs, :]
        qb = q.astype(BF16)
        kb = k.astype(BF16)
        qd = (q * qdec_ref[...]).astype(BF16)
        kd = k * kdec_ref[...]

        outs = []
        for hf in range(halves):
            sl = slice(hf * half_w, (hf + 1) * half_w)
            qh, kh, vh = qb[:, sl], kb[:, sl], v[:, sl]
            scores, vstack = [], []
            for j in range(heads_per_half):
                sel = lane_h == j
                s = _dot_nt(jnp.where(sel, qh, jnp.zeros_like(qh)), kh)
                scores.append((s * dmask_ref[hf * heads_per_half + j]).astype(BF16))
                vstack.append(jnp.where(sel, vh, jnp.zeros_like(vh)))
            inner = jnp.dot(jnp.concatenate(scores, axis=1), jnp.concatenate(vstack, axis=0),
                            preferred_element_type=F32)
            st = state_ref[hf]
            cross = jnp.dot(qd[:, sl], st.astype(BF16), preferred_element_type=F32)
            outs.append(inner + cross)
            upd = jnp.dot(kd[:, sl].T.astype(BF16), vh, preferred_element_type=F32)
            state_ref[hf] = st * cdec_ref[:, sl] + upd * bmask_ref[...]
        o = jnp.concatenate(outs, axis=1)

        d = o - group_mean(o)
        var = group_mean(d * d)
        on = d * lax.rsqrt(var + LN_EPS) * gnw_ref[...] + gnb_ref[...]
        g = g_ref[rs, :]
        o_ref[rs, :] = (g * _sigmoid(g) * on).astype(BF16)


def _retention_tables(seq, heads):
    dh = RET_HEAD_DIM
    c = RET_CHUNK
    half = dh // 2
    pos = jnp.arange(seq, dtype=F32)
    inv = ROPE_BASE ** (-jnp.arange(half, dtype=F32) / half)
    ang = pos[:, None] * inv[None, :]
    cos_t = jnp.tile(jnp.concatenate([jnp.cos(ang), jnp.cos(ang)], axis=1), (1, heads))
    sin_t = jnp.tile(jnp.concatenate([-jnp.sin(ang), jnp.sin(ang)], axis=1), (1, heads))
    log_gamma = jnp.log(1.0 - 2.0 ** (-5.0 - jnp.arange(heads, dtype=F32)))
    idx = jnp.arange(c, dtype=F32)
    rel = idx[:, None] - idx[None, :]
    dmask = jnp.where(rel >= 0, jnp.exp(log_gamma[:, None, None] * jnp.maximum(rel, 0.0)), 0.0)
    per_lane = lambda a: jnp.repeat(a, dh, axis=-1)
    qdec = per_lane(jnp.exp(log_gamma[None, :] * (idx[:, None] + 1.0)))
    kdec = per_lane(jnp.exp(log_gamma[None, :] * (c - 1.0 - idx[:, None])))
    cdec = per_lane(jnp.exp(log_gamma * c)[None, :])
    hid = jnp.arange(MXU_DIM) // dh
    same = hid[:, None] == hid[None, :]
    return cos_t, sin_t, qdec, kdec, dmask, cdec, jnp.where(same, 1.0 / dh, 0.0).astype(BF16), same.astype(F32)


def _retention_mixer(rq, rk, rv, rg, tables, gn_w, gn_b, layer, batch, seq):
    t, width = rq.shape
    heads = width // RET_HEAD_DIM
    c = RET_CHUNK
    rows = c * min(RET_STEPS, seq // c)
    n = seq // rows
    cos_t, sin_t, qdec, kdec, dmask, cdec, gmean, bmask = tables
    row = pl.BlockSpec((rows, width), lambda b, i: (b * n + i, 0))
    tab = pl.BlockSpec((rows, width), lambda b, i: (i, 0))
    halves = width // MXU_DIM
    return pl.pallas_call(
        functools.partial(_retention_kernel, heads=heads, chunk=c),
        grid=(batch, n),
        in_specs=[row, row, row, row, tab, tab, _resident((c, width)), _resident((c, width)),
                  _resident((heads, c, c)), _resident((1, width)), _layer_of(gn_w.shape, layer),
                  _layer_of(gn_b.shape, layer), _resident((MXU_DIM, MXU_DIM)), _resident((MXU_DIM, MXU_DIM))],
        out_specs=row,
        out_shape=jax.ShapeDtypeStruct((t, width), BF16),
        scratch_shapes=[pltpu.VMEM((halves, MXU_DIM, MXU_DIM), F32)],
        compiler_params=_cparams(("arbitrary", "arbitrary")),
        name="retention",
    )(rq, rk, rv, rg, cos_t, sin_t, qdec, kdec, dmask, cdec, gn_w, gn_b, gmean, bmask)


def _sb_kernel(q_ref, k_ref, v_ref, nw_ref, tri_ref, gm_ref, o_ref, acc_ref, run_ref, *, heads):
    tq, width = q_ref.shape
    tk = tq
    dh = width // heads
    i = pl.program_id(1)
    q = q_ref[...] * (dh ** -0.5)
    lane_q = lax.broadcasted_iota(jnp.int32, (tq, width), 1) // dh
    lane_k = lax.broadcasted_iota(jnp.int32, (tk, width), 1) // dh
    qms = [jnp.where(lane_q == h, q, jnp.zeros_like(q)) for h in range(heads)]
    causal = lax.broadcasted_iota(jnp.int32, (tq, tk), 1) < lax.broadcasted_iota(jnp.int32, (tq, tk), 0)

    def tile(j, masked):
        start = pl.multiple_of(j * tk, tk)
        kb = k_ref[pl.ds(start, tk), :]
        vb = v_ref[pl.ds(start, tk), :]
        weights, vstack, low = [], [], None
        for h in range(heads):
            z = _dot_nt(qms[h], kb)
            sp = jnp.maximum(z, 0.0) + jnp.log(1.0 + jnp.exp2(jnp.abs(z) * (-LOG2_E)))
            sp_sum = jnp.where(causal, sp, 0.0) if masked else sp
            hi, lo = _split_bf16(sp_sum)
            later = jnp.dot(jnp.concatenate([hi, lo], axis=1), tri_ref[...], preferred_element_type=F32)
            run = run_ref[h]
            a = jnp.exp2((z - sp - later - run) * LOG2_E)
            if masked:
                a = jnp.where(causal, a, 0.0)
            weights.append(a.astype(BF16))
            vstack.append(jnp.where(lane_k == h, vb, jnp.zeros_like(vb)))
            run = run + jnp.sum(sp_sum, axis=1, keepdims=True)
            run_ref[h] = run
            m = jnp.min(run)
            low = m if low is None else jnp.minimum(low, m)
        acc_ref[...] += jnp.dot(jnp.concatenate(weights, axis=1), jnp.concatenate(vstack, axis=0),
                                preferred_element_type=F32)
        return low

    acc_ref[...] = jnp.zeros_like(acc_ref)
    run_ref[...] = jnp.zeros_like(run_ref)
    low = tile(i, True)

    def cond(carry):
        j, low = carry
        return jnp.logical_and(j >= 0, low < SB_EXIT_LOG)

    def body(carry):
        j, _ = carry
        return j - 1, tile(j, False)

    lax.while_loop(cond, body, (i - 1, low))
    o = acc_ref[...]
    hi, lo = _split_bf16(o * o)
    ms = (jnp.dot(hi, gm_ref[...], preferred_element_type=F32) + jnp.dot(lo, gm_ref[...], preferred_element_type=F32))
    o_ref[...] = (o * lax.rsqrt(ms + LN_EPS) * nw_ref[...]).astype(BF16)


def _sb_mixer(sq, sk, sv, norm_w, layer, batch, seq):
    t, width = sq.shape
    heads = width // SB_HEAD_DIM
    tq = min(SB_TILE, seq)
    nq = seq // tq
    idx = jnp.arange(tq)
    tri = (idx[:, None] > idx[None, :]).astype(BF16)
    tri2 = jnp.concatenate([tri, tri], axis=0)
    hid = jnp.arange(width) // SB_HEAD_DIM
    gmean = jnp.where(hid[:, None] == hid[None, :], 1.0 / SB_HEAD_DIM, 0.0).astype(BF16)
    whole = pl.BlockSpec((None, seq, width), lambda b, i: (b, 0, 0))
    row = pl.BlockSpec((tq, width), lambda b, i: (b * nq + i, 0))
    return pl.pallas_call(
        functools.partial(_sb_kernel, heads=heads),
        grid=(batch, nq),
        in_specs=[row, whole, whole, _layer_of(norm_w.shape, layer), _resident((2 * tq, tq)),
                  _resident((width, width))],
        out_specs=row,
        out_shape=jax.ShapeDtypeStruct((t, width), BF16),
        scratch_shapes=[pltpu.VMEM((tq, width), F32), pltpu.VMEM((heads, tq, 1), F32)],
        compiler_params=_cparams(("parallel", "arbitrary")),
        name="stick_breaking",
    )(sq, sk.reshape(batch, seq, width), sv.reshape(batch, seq, width), norm_w, tri2, gmean)


def _mix_kernel(x_ref, ya_ref, yb_ref, yc_ref, w_ref, lw_ref, lb_ref, o_ref, *, alpha):
    wa = ya_ref.shape[1]
    wb = yb_ref.shape[1]
    mix = (jnp.dot(ya_ref[...], w_ref[0:wa, :], preferred_element_type=F32)
           + jnp.dot(yb_ref[...], w_ref[wa:wa + wb, :], preferred_element_type=F32)
           + jnp.dot(yc_ref[...], w_ref[wa + wb:, :], preferred_element_type=F32))
    o_ref[...] = _layer_norm_rows(alpha * x_ref[...] + mix, lw_ref[...], lb_ref[...])


def _mix(x2, ya, yb, yc, w_bf, ln_w, ln_b, layer, alpha, tm):
    t, d = x2.shape
    row = lambda w: pl.BlockSpec((tm, w), lambda i: (i, 0))
    return pl.pallas_call(
        functools.partial(_mix_kernel, alpha=alpha),
        grid=(t // tm,),
        in_specs=[row(d), row(ya.shape[1]), row(yb.shape[1]), row(yc.shape[1]), _layer_of(w_bf.shape, layer),
                  _layer_of(ln_w.shape, layer), _layer_of(ln_b.shape, layer)],
        out_specs=row(d),
        out_shape=jax.ShapeDtypeStruct((t, d), F32),
        compiler_params=_cparams(("parallel",)),
        name="mix_ln",
    )(x2, ya, yb, yc, w_bf, ln_w, ln_b)


def _swiglu_acc(xb, wg_ref, wu_ref, wd_ref, acc_ref, chunk, scale=None):
    ff = wg_ref.shape[-1]
    for c0 in range(0, ff, chunk):
        g = jnp.dot(xb, wg_ref[:, c0:c0 + chunk], preferred_element_type=F32)
        u = jnp.dot(xb, wu_ref[:, c0:c0 + chunk], preferred_element_type=F32)
        h = g * _sigmoid(g) * u
        if scale is not None:
            h = h * scale
        acc_ref[...] += jnp.dot(h.astype(BF16), wd_ref[c0:c0 + chunk, :], preferred_element_type=F32)


def _ffn_kernel(x_ref, wg_ref, wu_ref, wd_ref, lw_ref, lb_ref, o_ref, acc_ref, *, alpha):
    x = x_ref[...]
    acc_ref[...] = jnp.zeros_like(acc_ref)
    _swiglu_acc(x.astype(BF16), wg_ref, wu_ref, wd_ref, acc_ref, MXU_DIM)
    o_ref[...] = _layer_norm_rows(alpha * x + acc_ref[...], lw_ref[...], lb_ref[...])


def _pad_to(a, axis, mult):
    size = a.shape[axis]
    pad = (-size) % mult
    if pad == 0:
        return a
    widths = [(0, 0)] * a.ndim
    widths[axis] = (0, pad)
    return jnp.pad(a, widths)


def _ffn(x2, wg, wu, wd, ln_w, ln_b, li, layer, alpha, tm):
    t, d = x2.shape
    row = pl.BlockSpec((tm, d), lambda i: (i, 0))
    return pl.pallas_call(
        functools.partial(_ffn_kernel, alpha=alpha),
        grid=(t // tm,),
        in_specs=[row, _layer_of(wg.shape, li), _layer_of(wu.shape, li), _layer_of(wd.shape, li),
                  _layer_of(ln_w.shape, layer), _layer_of(ln_b.shape, layer)],
        out_specs=row,
        out_shape=jax.ShapeDtypeStruct((t, d), F32),
        scratch_shapes=[pltpu.VMEM((tm, d), F32)],
        compiler_params=_cparams(("parallel",)),
        name="ffn_ln",
    )(x2, wg, wu, wd, ln_w, ln_b)


def _router_kernel(x_ref, wh_ref, wl_ref, g_ref, *, n_experts):
    xh, xl = _split_bf16(x_ref[...])
    logits = (jnp.dot(xh, wh_ref[...], preferred_element_type=F32)
              + jnp.dot(xl, wh_ref[...], preferred_element_type=F32)
              + jnp.dot(xh, wl_ref[...], preferred_element_type=F32))
    lane = lax.broadcasted_iota(jnp.int32, logits.shape, 1).astype(F32)
    neg = -1e30
    logits = jnp.where(lane < n_experts, logits, neg)
    m1 = jnp.max(logits, axis=1, keepdims=True)
    i1 = jnp.min(jnp.where(logits == m1, lane, float(LANES)), axis=1, keepdims=True)
    rest = jnp.where(lane == i1, neg, logits)
    m2 = jnp.max(rest, axis=1, keepdims=True)
    i2 = jnp.min(jnp.where(rest == m2, lane, float(LANES)), axis=1, keepdims=True)
    e2 = jnp.exp(m2 - m1)
    w1 = 1.0 / (1.0 + e2)
    w2 = e2 * w1
    g_ref[...] = (jnp.where(lane == i1, w1, 0.0) + jnp.where(lane == i2, w2, 0.0)
                  + jnp.where(lane == n_experts, i1, 0.0) + jnp.where(lane == n_experts + 1, i2, 0.0)
                  + jnp.where(lane == n_experts + 2, w1, 0.0) + jnp.where(lane == n_experts + 3, w2, 0.0))


def _router(x2, wh, wl, li, n_experts, tm):
    t, d = x2.shape
    row = pl.BlockSpec((tm, d), lambda i: (i, 0))
    return pl.pallas_call(
        functools.partial(_router_kernel, n_experts=n_experts),
        grid=(t // tm,),
        in_specs=[row, _layer_of(wh.shape, li), _layer_of(wl.shape, li)],
        out_specs=pl.BlockSpec((tm, LANES), lambda i: (i, 0)),
        out_shape=jax.ShapeDtypeStruct((t, LANES), F32),
        compiler_params=_cparams(("parallel",)),
        name="router",
    )(x2, wh, wl)


def _moe_plan(sel, n_experts, tile):
    e_flat = sel.reshape(-1)
    onehot = (e_flat[None, :] == jnp.arange(n_experts)[:, None]).astype(jnp.int32)
    csum = jnp.cumsum(onehot, axis=1)
    counts = csum[:, -1]
    rank = jnp.sum(onehot * csum, axis=0) - 1
    padded = (counts + tile - 1) // tile * tile
    ends = jnp.cumsum(padded)
    starts = ends - padded
    pos = jnp.sum(onehot * starts[:, None], axis=0) + rank
    n_tiles = e_flat.shape[0] // tile + n_experts
    first_row = jnp.arange(n_tiles, dtype=jnp.int32)[:, None] * tile
    tile_expert = jnp.minimum(jnp.sum((first_row >= ends[None, :]).astype(jnp.int32), axis=1), n_experts - 1)
    return (pos.astype(jnp.int32), tile_expert.astype(jnp.int32), (ends[-1:] // tile).astype(jnp.int32),
            (starts + counts).astype(jnp.int32), ends.astype(jnp.int32))


def _dispatch_kernel(lo_ref, hi_ref, nu_ref, pos_ref, x_ref, xs_ref, zero_ref, sem, *, n_experts):
    tm = x_ref.shape[0]

    def row_copy(r, dst):
        return pltpu.make_async_copy(x_ref.at[pl.ds(r, 1)], xs_ref.at[pl.ds(dst, 1)], sem)

    def start(r, c):
        row_copy(r, pos_ref[2 * r]).start()
        row_copy(r, pos_ref[2 * r + 1]).start()
        return c

    lax.fori_loop(0, tm, start, 0, unroll=DMA_UNROLL)
    for _ in range(TOP_K):
        pltpu.make_async_copy(x_ref, xs_ref.at[pl.ds(0, tm)], sem).wait()

    @pl.when(pl.program_id(0) == 0)
    def _():
        zero_ref[...] = jnp.zeros_like(zero_ref)

        def zero_row(dst):
            return pltpu.make_async_copy(zero_ref.at[pl.ds(0, 1)], xs_ref.at[pl.ds(dst, 1)], sem)

        def zero_tile(i):
            return pltpu.make_async_copy(zero_ref, xs_ref.at[pl.ds(pl.multiple_of(i * tm, tm), tm)], sem)

        for e in range(n_experts):
            lax.fori_loop(lo_ref[e], hi_ref[e], lambda r, c: (zero_row(r).start(), c)[1], 0)
            lax.fori_loop(lo_ref[e], hi_ref[e], lambda r, c: (zero_row(0).wait(), c)[1], 0)
        n_tiles = xs_ref.shape[0] // tm
        lax.fori_loop(nu_ref[0], n_tiles, lambda i, c: (zero_tile(i).start(), c)[1], 0)
        lax.fori_loop(nu_ref[0], n_tiles, lambda i, c: (zero_tile(0).wait(), c)[1], 0)


def _dispatch(x2, pos, pad_lo, pad_hi, n_used, n_rows, tm):
    t, d = x2.shape
    n_experts = pad_lo.shape[0]
    grid_spec = pltpu.PrefetchScalarGridSpec(
        num_scalar_prefetch=3,
        grid=(t // tm,),
        in_specs=[pl.BlockSpec((TOP_K * tm,), lambda i, *_: (i,), memory_space=pltpu.SMEM),
                  pl.BlockSpec((tm, d), lambda i, *_: (i, 0))],
        out_specs=pl.BlockSpec(memory_space=pl.ANY),
        scratch_shapes=[pltpu.VMEM((tm, d), F32), pltpu.SemaphoreType.DMA(())],
    )
    return pl.pallas_call(
        functools.partial(_dispatch_kernel, n_experts=n_experts),
        grid_spec=grid_spec,
        out_shape=jax.ShapeDtypeStruct((n_rows, d), F32),
        compiler_params=_cparams(("arbitrary",)),
        name="moe_dispatch",
    )(pad_lo, pad_hi, n_used, pos, x2)


def _expert_kernel(te_ref, nu_ref, x_ref, wg_ref, wu_ref, wd_ref, y_ref, acc_ref):
    used = pl.program_id(0) < nu_ref[0]

    @pl.when(used)
    def _():
        acc_ref[...] = jnp.zeros_like(acc_ref)
        _swiglu_acc(x_ref[...].astype(BF16), wg_ref, wu_ref, wd_ref, acc_ref, MXU_DIM)
        y_ref[...] = acc_ref[...]

    @pl.when(jnp.logical_not(used))
    def _():
        y_ref[...] = jnp.zeros_like(y_ref)


def _experts(xs, tile_expert, n_used, wg, wu, wd, li, tile):
    n_rows, d = xs.shape
    ffp = wg.shape[3]
    row = pl.BlockSpec((tile, d), lambda i, te, nu: (i, 0))
    grid_spec = pltpu.PrefetchScalarGridSpec(
        num_scalar_prefetch=2,
        grid=(n_rows // tile,),
        in_specs=[row,
                  pl.BlockSpec((None, None, d, ffp), lambda i, te, nu: (li, te[i], 0, 0)),
                  pl.BlockSpec((None, None, d, ffp), lambda i, te, nu: (li, te[i], 0, 0)),
                  pl.BlockSpec((None, None, ffp, d), lambda i, te, nu: (li, te[i], 0, 0))],
        out_specs=row,
        scratch_shapes=[pltpu.VMEM((tile, d), F32)],
    )
    return pl.pallas_call(
        _expert_kernel,
        grid_spec=grid_spec,
        out_shape=jax.ShapeDtypeStruct((n_rows, d), F32),
        compiler_params=_cparams(("arbitrary",)),
        name="moe_experts",
    )(tile_expert, n_used, xs, wg, wu, wd)


def _combine_kernel(pos_ref, x_ref, g_ref, ys_ref, lw_ref, lb_ref, o_ref, y0_ref, y1_ref, sem, *, alpha, n_experts):
    tm = x_ref.shape[0]

    def row_copy(src, buf, r):
        return pltpu.make_async_copy(ys_ref.at[pl.ds(src, 1)], buf.at[pl.ds(r, 1)], sem)

    def start(r, c):
        row_copy(pos_ref[2 * r], y0_ref, r).start()
        row_copy(pos_ref[2 * r + 1], y1_ref, r).start()
        return c

    lax.fori_loop(0, tm, start, 0, unroll=DMA_UNROLL)
    for buf in (y0_ref, y1_ref):
        pltpu.make_async_copy(ys_ref.at[pl.ds(0, tm)], buf, sem).wait()
    g = g_ref[...]
    lane = lax.broadcasted_iota(jnp.int32, g.shape, 1)
    w0 = jnp.sum(jnp.where(lane == n_experts + 2, g, 0.0), axis=1, keepdims=True)
    w1 = jnp.sum(jnp.where(lane == n_experts + 3, g, 0.0), axis=1, keepdims=True)
    f = w0 * y0_ref[...] + w1 * y1_ref[...]
    o_ref[...] = _layer_norm_rows(alpha * x_ref[...] + f, lw_ref[...], lb_ref[...])


def _combine(x2, gates, pos, ys, ln_w, ln_b, layer, n_experts, alpha, tm):
    t, d = x2.shape
    row = pl.BlockSpec((tm, d), lambda i: (i, 0))
    return pl.pallas_call(
        functools.partial(_combine_kernel, alpha=alpha, n_experts=n_experts),
        grid=(t // tm,),
        in_specs=[pl.BlockSpec((TOP_K * tm,), lambda i: (i,), memory_space=pltpu.SMEM), row,
                  pl.BlockSpec((tm, LANES), lambda i: (i, 0)), pl.BlockSpec(memory_space=pl.ANY),
                  _layer_of(ln_w.shape, layer), _layer_of(ln_b.shape, layer)],
        out_specs=row,
        out_shape=jax.ShapeDtypeStruct((t, d), F32),
        scratch_shapes=[pltpu.VMEM((tm, d), F32), pltpu.VMEM((tm, d), F32), pltpu.SemaphoreType.DMA(())],
        compiler_params=_cparams(("arbitrary",)),
        name="moe_combine_ln",
    )(pos, x2, gates, ys, ln_w, ln_b)


def _moe(x2, gates, wg, wu, wd, ln_w, ln_b, li, layer, alpha, tm):
    t, d = x2.shape
    n_experts = wg.shape[1]
    sel = gates[:, n_experts:n_experts + TOP_K].astype(jnp.int32)
    pos, tile_expert, n_used, pad_lo, pad_hi = _moe_plan(sel, n_experts, tm)
    xs = _dispatch(x2, pos, pad_lo, pad_hi, n_used, tile_expert.shape[0] * tm, tm)
    ys = _experts(xs, tile_expert, n_used, wg, wu, wd, li, tm)
    return _combine(x2, gates, pos, ys, ln_w, ln_b, layer, n_experts, alpha, tm)


def kernel(x, w_in, ssm_lam_re, ssm_lam_im, ssm_log_dt, ssm_b_re, ssm_b_im, ssm_c_re, ssm_c_im, ssm_d, ssm_glu_w, ssm_glu_b, ssm_norm_w, ret_gn_w, ret_gn_b, sb_norm_w, w_out, ln_mix_w, ln_mix_b, ffn_w_gate, ffn_w_up, ffn_w_down, moe_router, moe_w_gate, moe_w_up, moe_w_down, ln_ffn_w, ln_ffn_b):
    batch, seq, d = x.shape
    depth = w_in.shape[0]
    alpha = (2.0 * depth) ** 0.25
    ssm_w = ssm_d.shape[1]
    ret_w = ret_gn_w.shape[1]
    sb_w = sb_norm_w.shape[1]
    t = batch * seq
    tm = min(ROW_TILE, t)
    vec = lambda a: a[:, None, :]

    w_in_bf, w_out_bf = w_in.astype(BF16), w_out.astype(BF16)
    glu_w_bf = ssm_glu_w.astype(BF16)
    ffn_g = _pad_to(ffn_w_gate, 2, MXU_DIM).astype(BF16)
    ffn_u = _pad_to(ffn_w_up, 2, MXU_DIM).astype(BF16)
    ffn_d = _pad_to(ffn_w_down, 1, MXU_DIM).astype(BF16)
    moe_g = _pad_to(moe_w_gate, 3, MXU_DIM).astype(BF16)
    moe_u = _pad_to(moe_w_up, 3, MXU_DIM).astype(BF16)
    moe_d = _pad_to(moe_w_down, 2, MXU_DIM).astype(BF16)
    n_experts = moe_router.shape[2]
    router_h, router_l = _split_bf16(_pad_to(moe_router, 2, LANES))
    table_in = _s5_table_inputs(ssm_lam_re, ssm_lam_im, ssm_log_dt, ssm_b_re, ssm_b_im, ssm_c_re, ssm_c_im, ssm_d)
    s5_mt, s5_wt, s5_vt = _s5_tables(*table_in[:7])
    lam_pos, lam_sup = table_in[7:]
    ret_tables = _retention_tables(seq, ret_w // RET_HEAD_DIM)
    s5_rows = min(S5_ROWS, seq // S5_CHUNK)

    h = x.reshape(t, d)
    for layer in range(depth):
        a, rq, rk, rv, rg, sq, sk, sv = _proj(h, w_in_bf, layer, ssm_w, ret_w, sb_w, tm)
        xp = _s5_states(a, s5_wt, lam_pos, lam_sup, layer, batch, s5_rows)
        y_ssm = _s5_outputs(a, s5_mt, xp, s5_vt, glu_w_bf, vec(ssm_glu_b), vec(ssm_norm_w), layer, s5_rows)
        y_ret = _retention_mixer(rq, rk, rv, rg, ret_tables, vec(ret_gn_w), vec(ret_gn_b), layer, batch, seq)
        y_sb = _sb_mixer(sq, sk, sv, vec(sb_norm_w), layer, batch, seq)
        h = _mix(h, y_ssm, y_ret, y_sb, w_out_bf, vec(ln_mix_w), vec(ln_mix_b), layer, alpha, tm)
        li = layer // 2
        if layer % 2 == 0:
            h = _ffn(h, ffn_g, ffn_u, ffn_d, vec(ln_ffn_w), vec(ln_ffn_b), li, layer, alpha, tm)
        else:
            gates = _router(h, router_h, router_l, li, n_experts, tm)
            h = _moe(h, gates, moe_g, moe_u, moe_d, vec(ln_ffn_w), vec(ln_ffn_b), li, layer, alpha, tm)
    return h.reshape(batch, seq, d)
```

```python
import functools
import math

import jax
import jax.numpy as jnp
from jax import lax
from jax.experimental import pallas as pl
from jax.experimental.pallas import tpu as pltpu

F32 = jnp.float32
BF16 = jnp.bfloat16

SSM_GROUP_CH = 16
SSM_STATE = 64
RET_HEAD_DIM = 64
SB_HEAD_DIM = 64
RET_CHUNK = 128
ROPE_BASE = 10000.0
TOP_K = 2
LN_EPS = 1e-5
LOG2_E = math.log2(math.e)

LANES = 128
SUBLANES = 8
MXU_DIM = 256
VMEM_LIMIT_BYTES = 56 * 1024 * 1024

ROW_TILE = 512
S5_CHUNK = 16
S5_PAIR_CH = 2 * SSM_GROUP_CH
S5_SUPER = 16
S5_ROWS = 256
RET_STEPS = 4
SB_TILE = 256
SB_EXIT_LOG = 110.0
DMA_UNROLL = 8


def _cparams(sem, vmem=VMEM_LIMIT_BYTES):
    return pltpu.CompilerParams(dimension_semantics=sem, vmem_limit_bytes=vmem)


def _resident(shape):
    nd = len(shape)
    return pl.BlockSpec(shape, lambda *_: (0,) * nd)


def _layer_of(shape, layer):
    nd = len(shape)
    return pl.BlockSpec((None,) + tuple(shape[1:]), lambda *_: (layer,) + (0,) * (nd - 1))


def _layer_norm_rows(h, w, b):
    mu = jnp.mean(h, axis=-1, keepdims=True)
    d = h - mu
    var = jnp.mean(d * d, axis=-1, keepdims=True)
    return d * lax.rsqrt(var + LN_EPS) * w + b


def _split_bf16(v):
    hi = v.astype(BF16)
    lo = (v - hi.astype(F32)).astype(BF16)
    return hi, lo


def _dot_nt(a, b):
    return lax.dot_general(a, b, (((1,), (1,)), ((), ())), preferred_element_type=F32)


def _sigmoid(v):
    return 1.0 / (1.0 + jnp.exp(-v))


def _row_permutation(groups, per_group):
    n = groups * per_group
    src = jnp.arange(n)
    dst = (src % per_group) * groups + src // per_group
    return (dst[:, None] == jnp.arange(n)[None, :]).astype(BF16).T


def _proj_kernel(x_ref, w_ref, perm_ref, a_ref, rq_ref, rk_ref, rv_ref, rg_ref,
                 sq_ref, sk_ref, sv_ref, *, ssm_w, ret_w, sb_w):
    xb = x_ref[...].astype(BF16)
    col = [0]

    def mm(width):
        c0 = col[0]
        col[0] = c0 + width
        return jnp.dot(xb, w_ref[:, c0:c0 + width], preferred_element_type=F32)

    u = mm(ssm_w).astype(BF16)
    up = jnp.dot(perm_ref[...], u, preferred_element_type=F32).astype(BF16)
    nch = u.shape[0] // S5_CHUNK
    for k in range(ssm_w // S5_PAIR_CH):
        a_ref[k] = jnp.concatenate(
            [up[s * nch:(s + 1) * nch, k * S5_PAIR_CH:(k + 1) * S5_PAIR_CH] for s in range(S5_CHUNK)], axis=1)
    rq_ref[...] = mm(ret_w).astype(BF16)
    rk_ref[...] = mm(ret_w).astype(BF16)
    rv_ref[...] = mm(ret_w).astype(BF16)
    rg_ref[...] = mm(ret_w).astype(BF16)
    sq_ref[...] = mm(sb_w).astype(BF16)
    sk_ref[...] = mm(sb_w).astype(BF16)
    sv_ref[...] = mm(sb_w).astype(BF16)


def _proj(x2, w_bf, layer, ssm_w, ret_w, sb_w, tm):
    t, d = x2.shape
    pairs = ssm_w // S5_PAIR_CH
    nch = tm // S5_CHUNK
    perm = _row_permutation(nch, S5_CHUNK)
    row = lambda w: pl.BlockSpec((tm, w), lambda i: (i, 0))
    out_shape = (
        jax.ShapeDtypeStruct((pairs, t // S5_CHUNK, S5_CHUNK * S5_PAIR_CH), BF16),
        jax.ShapeDtypeStruct((t, ret_w), BF16), jax.ShapeDtypeStruct((t, ret_w), BF16),
        jax.ShapeDtypeStruct((t, ret_w), BF16), jax.ShapeDtypeStruct((t, ret_w), BF16),
        jax.ShapeDtypeStruct((t, sb_w), BF16), jax.ShapeDtypeStruct((t, sb_w), BF16),
        jax.ShapeDtypeStruct((t, sb_w), BF16),
    )
    out_specs = (pl.BlockSpec((pairs, nch, S5_CHUNK * S5_PAIR_CH), lambda i: (0, i, 0)),
                 row(ret_w), row(ret_w), row(ret_w), row(ret_w), row(sb_w), row(sb_w), row(sb_w))
    return pl.pallas_call(
        functools.partial(_proj_kernel, ssm_w=ssm_w, ret_w=ret_w, sb_w=sb_w),
        grid=(t // tm,),
        in_specs=[row(d), _layer_of(w_bf.shape, layer), _resident(perm.shape)],
        out_specs=out_specs,
        out_shape=out_shape,
        compiler_params=_cparams(("parallel",)),
        name="proj",
    )(x2, w_bf, perm)


def _s5_table_inputs(lam_re, lam_im, log_dt, b_re, b_im, c_re, c_im, d_skip):
    nl, g, p = lam_re.shape
    hch = b_re.shape[-1]
    pairs = g // 2
    lc = S5_CHUNK
    dt = jnp.exp(log_dt)[..., None]
    a_re, a_im = lam_re * dt, lam_im * dt

    def lam_pow(n):
        n = n[None, :, None, None]
        mag = jnp.exp(n * a_re[:, None])
        return mag * jnp.cos(n * a_im[:, None]), mag * jnp.sin(n * a_im[:, None])

    lb_re, lb_im = lam_pow(jnp.ones((1,), F32))
    lb_re, lb_im = lb_re[:, 0], lb_im[:, 0]
    den = lam_re * lam_re + lam_im * lam_im
    z_re = ((lb_re - 1.0) * lam_re + lb_im * lam_im) / den
    z_im = (lb_im * lam_re - (lb_re - 1.0) * lam_im) / den
    bb_re = z_re[..., None] * b_re - z_im[..., None] * b_im
    bb_im = z_re[..., None] * b_im + z_im[..., None] * b_re
    eye2 = jnp.eye(2, dtype=F32)

    def b_blocks(v):
        v = v.reshape(nl, pairs, 2, p, hch)
        return jnp.einsum('lkgph,gf->lkghfp', v, eye2).reshape(nl, pairs, 2 * hch, 2 * p)

    def c_blocks(v):
        v = v.reshape(nl, pairs, 2, hch, p)
        return jnp.einsum('lkghp,gf->lkghfp', v, eye2).reshape(nl, pairs, 2 * hch, 2 * p)

    pw_re, pw_im = lam_pow(jnp.arange(lc + 1, dtype=F32))
    pad = (-(lc + 1)) % SUBLANES

    def pair_rows(v):
        v = v.reshape(nl, lc + 1, pairs, 2 * p).transpose(0, 2, 1, 3)
        return jnp.pad(v, ((0, 0), (0, 0), (0, pad), (0, 0)))

    dd = d_skip.reshape(nl, pairs, 2 * hch)[..., None] * jnp.eye(2 * hch, lc * 2 * hch, dtype=F32)

    def pair_lanes(re, im):
        k = re.shape[1]
        v = jnp.stack([re.reshape(nl, k, pairs, 2 * p), im.reshape(nl, k, pairs, 2 * p)], axis=3)
        return v.reshape(nl, k, pairs * 4 * p)

    lam_pos = pair_lanes(*lam_pow(lc * jnp.arange(S5_SUPER, dtype=F32)))
    lam_sup = pair_lanes(*lam_pow(jnp.full((1,), float(lc * S5_SUPER), F32)))
    return (b_blocks(bb_re), b_blocks(bb_im), c_blocks(c_re), c_blocks(c_im), pair_rows(pw_re), pair_rows(pw_im),
            dd, lam_pos, lam_sup)


def _s5_table_kernel(bbr_ref, bbi_ref, ctr_ref, cti_ref, pwr_ref, pwi_ref, dd_ref, mt_ref, wt_ref, vt_ref):
    lc = S5_CHUNK
    bbr, bbi, ctr, cti = bbr_ref[...], bbi_ref[...], ctr_ref[...], cti_ref[...]
    ch = bbr.shape[0]
    pwr = [pwr_ref[l:l + 1, :] for l in range(lc + 1)]
    pwi = [pwi_ref[l:l + 1, :] for l in range(lc + 1)]
    clr = [ctr * pwr[l] - cti * pwi[l] for l in range(lc + 1)]
    cli = [ctr * pwi[l] + cti * pwr[l] for l in range(lc + 1)]

    def dot3(a, b):
        ah, al = _split_bf16(a)
        bh, bl = _split_bf16(b)
        return _dot_nt(ah, bh) + _dot_nt(al, bh) + _dot_nt(ah, bl)

    r0 = (dot3(bbr, jnp.concatenate(clr[:lc], axis=0)) - dot3(bbi, jnp.concatenate(cli[:lc], axis=0))
          + dd_ref[...])
    lane = lax.broadcasted_iota(jnp.int32, r0.shape, 1)
    for s in range(lc):
        blk = r0 if s == 0 else jnp.where(lane >= s * ch, pltpu.roll(r0, s * ch, axis=1), 0.0)
        mt_ref[s * ch:(s + 1) * ch, :] = blk.astype(BF16)
        l = lc - 1 - s
        wt_ref[s * ch:(s + 1) * ch, :] = jnp.concatenate(
            [bbr * pwr[l] - bbi * pwi[l], bbr * pwi[l] + bbi * pwr[l]], axis=1).astype(BF16)
    vt_ref[...] = jnp.concatenate([jnp.concatenate(clr[1:], axis=0), -jnp.concatenate(cli[1:], axis=0)],
                                  axis=1).astype(BF16)


def _s5_tables(bbr, bbi, ctr, cti, pwr, pwi, dd):
    nl, pairs, ch, sdim = bbr.shape
    kdim = S5_CHUNK * ch
    blk = lambda a: pl.BlockSpec((None, None) + a.shape[2:], lambda l, k: (l, k, 0, 0))
    out = lambda w: pl.BlockSpec((None, None, kdim, w), lambda l, k: (l, k, 0, 0))
    return pl.pallas_call(
        _s5_table_kernel,
        grid=(nl, pairs),
        in_specs=[blk(bbr), blk(bbi), blk(ctr), blk(cti), blk(pwr), blk(pwi), blk(dd)],
        out_specs=(out(kdim), out(2 * sdim), out(2 * sdim)),
        out_shape=(jax.ShapeDtypeStruct((nl, pairs, kdim, kdim), BF16),
                   jax.ShapeDtypeStruct((nl, pairs, kdim, 2 * sdim), BF16),
                   jax.ShapeDtypeStruct((nl, pairs, kdim, 2 * sdim), BF16)),
        compiler_params=_cparams(("parallel", "parallel")),
        name="s5_tables",
    )(bbr, bbi, ctr, cti, pwr, pwi, dd)


def _cmul(l, x):
    out = []
    for c0 in range(0, x.shape[1], 2 * LANES):
        lr, li = l[:, c0:c0 + LANES], l[:, c0 + LANES:c0 + 2 * LANES]
        xr, xi = x[:, c0:c0 + LANES], x[:, c0 + LANES:c0 + 2 * LANES]
        out += [lr * xr - li * xi, lr * xi + li * xr]
    return jnp.concatenate(out, axis=1)


def _s5_state_kernel(a_ref, wt_ref, lpos_ref, lsup_ref, perm_ref, permt_ref, xp_ref,
                     carry_ref, sin_ref, loc_ref, sup_ref, *, pairs, rows):
    @pl.when(pl.program_id(1) == 0)
    def _():
        carry_ref[...] = jnp.zeros_like(carry_ref)

    ns = rows // S5_SUPER
    blk = 2 * LANES
    perm = perm_ref[...]
    for k in range(pairs):
        ap = jnp.dot(perm, a_ref[k], preferred_element_type=F32).astype(BF16)
        sin_ref[:, k * blk:(k + 1) * blk] = jnp.dot(ap, wt_ref[k], preferred_element_type=F32)

    lam_chunk = lpos_ref[1:2, :]
    x = jnp.zeros((ns, sin_ref.shape[1]), F32)
    for j in range(S5_SUPER):
        loc_ref[j * ns:(j + 1) * ns, :] = x
        x = _cmul(lam_chunk, x) + sin_ref[j * ns:(j + 1) * ns, :]
    carry = carry_ref[...]
    lam_sup = lsup_ref[...]
    for m in range(ns):
        sup_ref[m:m + 1, :] = carry
        carry = _cmul(lam_sup, carry) + x[m:m + 1, :]
    carry_ref[...] = carry
    sup = sup_ref[...]
    for j in range(S5_SUPER):
        loc_ref[j * ns:(j + 1) * ns, :] += _cmul(lpos_ref[j:j + 1, :], sup)
    xp_ref[...] = jnp.dot(permt_ref[...], loc_ref[...].astype(BF16), preferred_element_type=F32).astype(BF16)


def _s5_states(a, wt, lam_pos, lam_sup, layer, batch, rows):
    pairs, n, kdim = a.shape
    per_b = n // batch // rows
    width = lam_pos.shape[2]
    ns = rows // S5_SUPER
    perm = _row_permutation(ns, S5_SUPER)
    return pl.pallas_call(
        functools.partial(_s5_state_kernel, pairs=pairs, rows=rows),
        grid=(batch, per_b),
        in_specs=[pl.BlockSpec((pairs, rows, kdim), lambda b, i: (0, b * per_b + i, 0)),
                  _layer_of(wt.shape, layer), _layer_of(lam_pos.shape, layer), _layer_of(lam_sup.shape, layer),
                  _resident(perm.shape), _resident(perm.shape)],
        out_specs=pl.BlockSpec((rows, width), lambda b, i: (b * per_b + i, 0)),
        out_shape=jax.ShapeDtypeStruct((n, width), BF16),
        scratch_shapes=[pltpu.VMEM((1, width), F32), pltpu.VMEM((rows, width), F32),
                        pltpu.VMEM((rows, width), F32), pltpu.VMEM((ns, width), F32)],
        compiler_params=_cparams(("arbitrary", "arbitrary")),
        name="s5_states",
    )(a, wt, lam_pos, lam_sup, perm, perm.T)


def _s5_out_kernel(a_ref, mt_ref, xp_ref, vt_ref, gw_ref, gb_ref, nw_ref, gm_ref, permt_ref, o_ref, *, pairs):
    rows = a_ref.shape[1]
    ch = S5_PAIR_CH
    sdim = vt_ref.shape[2]
    ys = [jnp.dot(a_ref[k], mt_ref[k], preferred_element_type=F32)
          + _dot_nt(xp_ref[:, k * sdim:(k + 1) * sdim], vt_ref[k]) for k in range(pairs)]
    outs = []
    for t in range(S5_CHUNK):
        y = jnp.concatenate([ys[k][:, t * ch:(t + 1) * ch] for k in range(pairs)], axis=1)
        g = y * (0.5 * (1.0 + jnp.tanh(math.sqrt(2.0 / math.pi) * (y + 0.044715 * (y * y * y)))))
        z = jnp.dot(g.astype(BF16), gw_ref[...], preferred_element_type=F32) + gb_ref[...]
        out = g * _sigmoid(z)
        hi, lo = _split_bf16(out * out)
        ms = (jnp.dot(hi, gm_ref[...], preferred_element_type=F32)
              + jnp.dot(lo, gm_ref[...], preferred_element_type=F32))
        outs.append((out * lax.rsqrt(ms + LN_EPS) * nw_ref[...]).astype(BF16))
    gch = permt_ref.shape[0] // S5_CHUNK
    for gi in range(rows // gch):
        stack = jnp.concatenate([outs[t][gi * gch:(gi + 1) * gch, :] for t in range(S5_CHUNK)], axis=0)
        o_ref[gi * gch * S5_CHUNK:(gi + 1) * gch * S5_CHUNK, :] = jnp.dot(
            permt_ref[...], stack, preferred_element_type=F32).astype(BF16)


def _s5_outputs(a, mt, xp, vt, glu_w, glu_b, norm_w, layer, rows):
    pairs, n, kdim = a.shape
    w = glu_w.shape[1]
    gch = min(ROW_TILE // S5_CHUNK, rows)
    groups = jnp.arange(w) // SSM_GROUP_CH
    gmean = jnp.where(groups[:, None] == groups[None, :], 1.0 / SSM_GROUP_CH, 0.0).astype(BF16)
    permt = _row_permutation(gch, S5_CHUNK).T
    return pl.pallas_call(
        functools.partial(_s5_out_kernel, pairs=pairs),
        grid=(n // rows,),
        in_specs=[pl.BlockSpec((pairs, rows, kdim), lambda i: (0, i, 0)), _layer_of(mt.shape, layer),
                  pl.BlockSpec((rows, xp.shape[1]), lambda i: (i, 0)), _layer_of(vt.shape, layer),
                  _layer_of(glu_w.shape, layer), _layer_of(glu_b.shape, layer), _layer_of(norm_w.shape, layer),
                  _resident(gmean.shape), _resident(permt.shape)],
        out_specs=pl.BlockSpec((rows * S5_CHUNK, w), lambda i: (i, 0)),
        out_shape=jax.ShapeDtypeStruct((n * S5_CHUNK, w), BF16),
        compiler_params=_cparams(("parallel",)),
        name="s5_outputs",
    )(a, mt, xp, vt, glu_w, glu_b, norm_w, gmean, permt)


def _retention_kernel(q_ref, k_ref, v_ref, g_ref, cos_ref, sin_ref, qdec_ref, kdec_ref, dmask_ref,
                      cdec_ref, gnw_ref, gnb_ref, gm_ref, bmask_ref, o_ref, state_ref, *, heads, chunk):
    width = q_ref.shape[1]
    c = chunk
    half_w = MXU_DIM
    halves = width // half_w
    heads_per_half = heads // halves
    dh = width // heads

    @pl.when(pl.program_id(1) == 0)
    def _():
        state_ref[...] = jnp.zeros_like(state_ref)

    lane = lax.broadcasted_iota(jnp.int32, (c, width), 1)
    first_half = (lane % dh) < (dh // 2)
    lane_h = lax.broadcasted_iota(jnp.int32, (c, half_w), 1) // dh

    def group_mean(a):
        hi, lo = _split_bf16(a)
        parts = []
        for hf in range(halves):
            sl = slice(hf * half_w, (hf + 1) * half_w)
            parts.append(jnp.dot(hi[:, sl], gm_ref[...], preferred_element_type=F32)
                         + jnp.dot(lo[:, sl], gm_ref[...], preferred_element_type=F32))
        return jnp.concatenate(parts, axis=1)

    for step in range(q_ref.shape[0] // c):
        rs = slice(step * c, (step + 1) * c)

        def rope(x):
            swapped = jnp.where(first_half, pltpu.roll(x, width - dh // 2, axis=1), pltpu.roll(x, dh // 2, axis=1))
            return x * cos_ref[rs, :] + swapped * sin_ref[rs, :]

        q = rope(q_ref[rs, :].astype(F32))
        k = rope(k_ref[rs, :].astype(F32)) * (dh ** -0.5)
        v = v_ref[rs, :]
        qb = q.astype(BF16)
        kb = k.astype(BF16)
        qd = (q * qdec_ref[...]).astype(BF16)
        kd = k * kdec_ref[...]

        outs = []
        for hf in range(halves):
            sl = slice(hf * half_w, (hf + 1) * half_w)
            qh, kh, vh = qb[:, sl], kb[:, sl], v[:, sl]
            scores, vstack = [], []
            for j in range(heads_per_half):
                sel = lane_h == j
                s = _dot_nt(jnp.where(sel, qh, jnp.zeros_like(qh)), kh)
                scores.append((s * dmask_ref[hf * heads_per_half + j]).astype(BF16))
                vstack.append(jnp.where(sel, vh, jnp.zeros_like(vh)))
            inner = jnp.dot(jnp.concatenate(scores, axis=1), jnp.concatenate(vstack, axis=0),
                            preferred_element_type=F32)
            st = state_ref[hf]
            cross = jnp.dot(qd[:, sl], st.astype(BF16), preferred_element_type=F32)
            outs.append(inner + cross)
            upd = jnp.dot(kd[:, sl].T.astype(BF16), vh, preferred_element_type=F32)
            state_ref[hf] = st * cdec_ref[:, sl] + upd * bmask_ref[...]
        o = jnp.concatenate(outs, axis=1)

        d = o - group_mean(o)
        var = group_mean(d * d)
        on = d * lax.rsqrt(var + LN_EPS) * gnw_ref[...] + gnb_ref[...]
        g = g_ref[rs, :].astype(F32)
        o_ref[rs, :] = (g * _sigmoid(g) * on).astype(BF16)


def _retention_tables(seq, heads):
    dh = RET_HEAD_DIM
    c = RET_CHUNK
    half = dh // 2
    pos = jnp.arange(seq, dtype=F32)
    inv = ROPE_BASE ** (-jnp.arange(half, dtype=F32) / half)
    ang = pos[:, None] * inv[None, :]
    cos_t = jnp.tile(jnp.concatenate([jnp.cos(ang), jnp.cos(ang)], axis=1), (1, heads))
    sin_t = jnp.tile(jnp.concatenate([-jnp.sin(ang), jnp.sin(ang)], axis=1), (1, heads))
    log_gamma = jnp.log(1.0 - 2.0 ** (-5.0 - jnp.arange(heads, dtype=F32)))
    idx = jnp.arange(c, dtype=F32)
    rel = idx[:, None] - idx[None, :]
    dmask = jnp.where(rel >= 0, jnp.exp(log_gamma[:, None, None] * jnp.maximum(rel, 0.0)), 0.0)
    per_lane = lambda a: jnp.repeat(a, dh, axis=-1)
    qdec = per_lane(jnp.exp(log_gamma[None, :] * (idx[:, None] + 1.0)))
    kdec = per_lane(jnp.exp(log_gamma[None, :] * (c - 1.0 - idx[:, None])))
    cdec = per_lane(jnp.exp(log_gamma * c)[None, :])
    hid = jnp.arange(MXU_DIM) // dh
    same = hid[:, None] == hid[None, :]
    return cos_t, sin_t, qdec, kdec, dmask, cdec, jnp.where(same, 1.0 / dh, 0.0).astype(BF16), same.astype(F32)


def _retention_mixer(rq, rk, rv, rg, tables, gn_w, gn_b, layer, batch, seq):
    t, width = rq.shape
    heads = width // RET_HEAD_DIM
    c = RET_CHUNK
    rows = c * min(RET_STEPS, seq // c)
    n = seq // rows
    cos_t, sin_t, qdec, kdec, dmask, cdec, gmean, bmask = tables
    row = pl.BlockSpec((rows, width), lambda b, i: (b * n + i, 0))
    tab = pl.BlockSpec((rows, width), lambda b, i: (i, 0))
    halves = width // MXU_DIM
    return pl.pallas_call(
        functools.partial(_retention_kernel, heads=heads, chunk=c),
        grid=(batch, n),
        in_specs=[row, row, row, row, tab, tab, _resident((c, width)), _resident((c, width)),
                  _resident((heads, c, c)), _resident((1, width)), _layer_of(gn_w.shape, layer),
                  _layer_of(gn_b.shape, layer), _resident((MXU_DIM, MXU_DIM)), _resident((MXU_DIM, MXU_DIM))],
        out_specs=row,
        out_shape=jax.ShapeDtypeStruct((t, width), BF16),
        scratch_shapes=[pltpu.VMEM((halves, MXU_DIM, MXU_DIM), F32)],
        compiler_params=_cparams(("arbitrary", "arbitrary")),
        name="retention",
    )(rq, rk, rv, rg, cos_t, sin_t, qdec, kdec, dmask, cdec, gn_w, gn_b, gmean, bmask)


def _sb_kernel(q_ref, k_ref, v_ref, nw_ref, tri_ref, gm_ref, o_ref, acc_ref, run_ref, *, heads):
    tq, width = q_ref.shape
    tk = tq
    dh = width // heads
    i = pl.program_id(1)
    q = q_ref[...] * (dh ** -0.5)
    lane_q = lax.broadcasted_iota(jnp.int32, (tq, width), 1) // dh
    lane_k = lax.broadcasted_iota(jnp.int32, (tk, width), 1) // dh
    qms = [jnp.where(lane_q == h, q, jnp.zeros_like(q)) for h in range(heads)]
    causal = lax.broadcasted_iota(jnp.int32, (tq, tk), 1) < lax.broadcasted_iota(jnp.int32, (tq, tk), 0)

    def tile(j, masked):
        start = pl.multiple_of(j * tk, tk)
        kb = k_ref[pl.ds(start, tk), :]
        vb = v_ref[pl.ds(start, tk), :]
        weights, vstack, low = [], [], None
        for h in range(heads):
            z = _dot_nt(qms[h], kb)
            sp = jnp.maximum(z, 0.0) + jnp.log(1.0 + jnp.exp2(jnp.abs(z) * (-LOG2_E)))
            sp_sum = jnp.where(causal, sp, 0.0) if masked else sp
            hi, lo = _split_bf16(sp_sum)
            later = jnp.dot(jnp.concatenate([hi, lo], axis=1), tri_ref[...], preferred_element_type=F32)
            run = run_ref[h]
            a = jnp.exp2((z - sp - later - run) * LOG2_E)
            if masked:
                a = jnp.where(causal, a, 0.0)
            weights.append(a.astype(BF16))
            vstack.append(jnp.where(lane_k == h, vb, jnp.zeros_like(vb)))
            run = run + jnp.sum(sp_sum, axis=1, keepdims=True)
            run_ref[h] = run
            m = jnp.min(run)
            low = m if low is None else jnp.minimum(low, m)
        acc_ref[...] += jnp.dot(jnp.concatenate(weights, axis=1), jnp.concatenate(vstack, axis=0),
                                preferred_element_type=F32)
        return low

    acc_ref[...] = jnp.zeros_like(acc_ref)
    run_ref[...] = jnp.zeros_like(run_ref)
    low = tile(i, True)

    def cond(carry):
        j, low = carry
        return jnp.logical_and(j >= 0, low < SB_EXIT_LOG)

    def body(carry):
        j, _ = carry
        return j - 1, tile(j, False)

    lax.while_loop(cond, body, (i - 1, low))
    o = acc_ref[...]
    hi, lo = _split_bf16(o * o)
    ms = (jnp.dot(hi, gm_ref[...], preferred_element_type=F32) + jnp.dot(lo, gm_ref[...], preferred_element_type=F32))
    o_ref[...] = (o * lax.rsqrt(ms + LN_EPS) * nw_ref[...]).astype(BF16)


def _sb_mixer(sq, sk, sv, norm_w, layer, batch, seq):
    t, width = sq.shape
    heads = width // SB_HEAD_DIM
    tq = min(SB_TILE, seq)
    nq = seq // tq
    idx = jnp.arange(tq)
    tri = (idx[:, None] > idx[None, :]).astype(BF16)
    tri2 = jnp.concatenate([tri, tri], axis=0)
    hid = jnp.arange(width) // SB_HEAD_DIM
    gmean = jnp.where(hid[:, None] == hid[None, :], 1.0 / SB_HEAD_DIM, 0.0).astype(BF16)
    whole = pl.BlockSpec((None, seq, width), lambda b, i: (b, 0, 0))
    row = pl.BlockSpec((tq, width), lambda b, i: (b * nq + i, 0))
    return pl.pallas_call(
        functools.partial(_sb_kernel, heads=heads),
        grid=(batch, nq),
        in_specs=[row, whole, whole, _layer_of(norm_w.shape, layer), _resident((2 * tq, tq)),
                  _resident((width, width))],
        out_specs=row,
        out_shape=jax.ShapeDtypeStruct((t, width), BF16),
        scratch_shapes=[pltpu.VMEM((tq, width), F32), pltpu.VMEM((heads, tq, 1), F32)],
        compiler_params=_cparams(("parallel", "arbitrary")),
        name="stick_breaking",
    )(sq, sk.reshape(batch, seq, width), sv.reshape(batch, seq, width), norm_w, tri2, gmean)


def _top2_gates(x, w_hl, n_experts):
    xh, xl = _split_bf16(x)
    both = jnp.dot(xh, w_hl, preferred_element_type=F32)
    logits = both[:, :LANES] + both[:, LANES:] + jnp.dot(xl, w_hl[:, :LANES], preferred_element_type=F32)
    lane = lax.broadcasted_iota(jnp.int32, logits.shape, 1).astype(F32)
    neg = -1e30
    logits = jnp.where(lane < n_experts, logits, neg)
    m1 = jnp.max(logits, axis=1, keepdims=True)
    i1 = jnp.min(jnp.where(logits == m1, lane, float(LANES)), axis=1, keepdims=True)
    rest = jnp.where(lane == i1, neg, logits)
    m2 = jnp.max(rest, axis=1, keepdims=True)
    i2 = jnp.min(jnp.where(rest == m2, lane, float(LANES)), axis=1, keepdims=True)
    e2 = jnp.exp(m2 - m1)
    w1 = 1.0 / (1.0 + e2)
    w2 = e2 * w1
    return (jnp.where(lane == i1, w1, 0.0) + jnp.where(lane == i2, w2, 0.0)
            + jnp.where(lane == n_experts, i1, 0.0) + jnp.where(lane == n_experts + 1, i2, 0.0)
            + jnp.where(lane == n_experts + 2, w1, 0.0) + jnp.where(lane == n_experts + 3, w2, 0.0))


def _mix_kernel(x_ref, ya_ref, yb_ref, yc_ref, w_ref, lw_ref, lb_ref, *rest, alpha, n_experts):
    wa = ya_ref.shape[1]
    wb = yb_ref.shape[1]
    mix = (jnp.dot(ya_ref[...], w_ref[0:wa, :], preferred_element_type=F32)
           + jnp.dot(yb_ref[...], w_ref[wa:wa + wb, :], preferred_element_type=F32)
           + jnp.dot(yc_ref[...], w_ref[wa + wb:, :], preferred_element_type=F32))
    out = _layer_norm_rows(alpha * x_ref[...] + mix, lw_ref[...], lb_ref[...])
    if n_experts:
        whl_ref, o_ref, g_ref = rest
        g_ref[...] = _top2_gates(out, whl_ref[...], n_experts)
    else:
        (o_ref,) = rest
    o_ref[...] = out


def _mix(x2, ya, yb, yc, w_bf, ln_w, ln_b, layer, alpha, tm, router=None):
    t, d = x2.shape
    row = lambda w: pl.BlockSpec((tm, w), lambda i: (i, 0))
    in_specs = [row(d), row(ya.shape[1]), row(yb.shape[1]), row(yc.shape[1]), _layer_of(w_bf.shape, layer),
                _layer_of(ln_w.shape, layer), _layer_of(ln_b.shape, layer)]
    args = [x2, ya, yb, yc, w_bf, ln_w, ln_b]
    out_specs, out_shape, n_experts = row(d), jax.ShapeDtypeStruct((t, d), F32), 0
    if router is not None:
        w_hl, li, n_experts = router
        in_specs += [_layer_of(w_hl.shape, li)]
        args += [w_hl]
        out_specs = (out_specs, row(LANES))
        out_shape = (out_shape, jax.ShapeDtypeStruct((t, LANES), F32))
    return pl.pallas_call(
        functools.partial(_mix_kernel, alpha=alpha, n_experts=n_experts),
        grid=(t // tm,),
        in_specs=in_specs,
        out_specs=out_specs,
        out_shape=out_shape,
        compiler_params=_cparams(("parallel",)),
        name="mix_ln",
    )(*args)


def _swiglu_acc(xb, wg_ref, wu_ref, wd_ref, acc_ref, chunk, scale=None):
    ff = wg_ref.shape[-1]
    for c0 in range(0, ff, chunk):
        g = jnp.dot(xb, wg_ref[:, c0:c0 + chunk], preferred_element_type=F32)
        u = jnp.dot(xb, wu_ref[:, c0:c0 + chunk], preferred_element_type=F32)
        h = g * _sigmoid(g) * u
        if scale is not None:
            h = h * scale
        acc_ref[...] += jnp.dot(h.astype(BF16), wd_ref[c0:c0 + chunk, :], preferred_element_type=F32)


def _ffn_kernel(x_ref, wg_ref, wu_ref, wd_ref, lw_ref, lb_ref, o_ref, acc_ref, *, alpha):
    x = x_ref[...]
    acc_ref[...] = jnp.zeros_like(acc_ref)
    _swiglu_acc(x.astype(BF16), wg_ref, wu_ref, wd_ref, acc_ref, MXU_DIM)
    o_ref[...] = _layer_norm_rows(alpha * x + acc_ref[...], lw_ref[...], lb_ref[...])


def _pad_to(a, axis, mult):
    size = a.shape[axis]
    pad = (-size) % mult
    if pad == 0:
        return a
    widths = [(0, 0)] * a.ndim
    widths[axis] = (0, pad)
    return jnp.pad(a, widths)


def _ffn(x2, wg, wu, wd, ln_w, ln_b, li, layer, alpha, tm):
    t, d = x2.shape
    row = pl.BlockSpec((tm, d), lambda i: (i, 0))
    return pl.pallas_call(
        functools.partial(_ffn_kernel, alpha=alpha),
        grid=(t // tm,),
        in_specs=[row, _layer_of(wg.shape, li), _layer_of(wu.shape, li), _layer_of(wd.shape, li),
                  _layer_of(ln_w.shape, layer), _layer_of(ln_b.shape, layer)],
        out_specs=row,
        out_shape=jax.ShapeDtypeStruct((t, d), F32),
        scratch_shapes=[pltpu.VMEM((tm, d), F32)],
        compiler_params=_cparams(("parallel",)),
        name="ffn_ln",
    )(x2, wg, wu, wd, ln_w, ln_b)


def _moe_plan(sel, n_experts, tile):
    e_flat = sel.reshape(-1)
    onehot = (e_flat[None, :] == jnp.arange(n_experts)[:, None]).astype(jnp.int32)
    csum = jnp.cumsum(onehot, axis=1)
    counts = csum[:, -1]
    rank = jnp.sum(onehot * csum, axis=0) - 1
    padded = (counts + tile - 1) // tile * tile
    ends = jnp.cumsum(padded)
    starts = ends - padded
    pos = jnp.sum(onehot * starts[:, None], axis=0) + rank
    n_tiles = e_flat.shape[0] // tile + n_experts
    first_row = jnp.arange(n_tiles, dtype=jnp.int32)[:, None] * tile
    tile_expert = jnp.minimum(jnp.sum((first_row >= ends[None, :]).astype(jnp.int32), axis=1), n_experts - 1)
    return (pos.astype(jnp.int32), tile_expert.astype(jnp.int32), (ends[-1:] // tile).astype(jnp.int32),
            (starts + counts).astype(jnp.int32), ends.astype(jnp.int32))


def _dispatch_kernel(lo_ref, hi_ref, nu_ref, pos_ref, x_ref, xs_ref, zero_ref, sem, *, n_experts):
    tm = x_ref.shape[0]

    def row_copy(r, dst):
        return pltpu.make_async_copy(x_ref.at[pl.ds(r, 1)], xs_ref.at[pl.ds(dst, 1)], sem)

    def start(r, c):
        row_copy(r, pos_ref[2 * r]).start(priority=0)
        row_copy(r, pos_ref[2 * r + 1]).start(priority=1)
        return c

    lax.fori_loop(0, tm, start, 0, unroll=DMA_UNROLL)
    for _ in range(TOP_K):
        pltpu.make_async_copy(x_ref, xs_ref.at[pl.ds(0, tm)], sem).wait()

    @pl.when(pl.program_id(0) == 0)
    def _():
        zero_ref[...] = jnp.zeros_like(zero_ref)

        def zero_row(dst):
            return pltpu.make_async_copy(zero_ref.at[pl.ds(0, 1)], xs_ref.at[pl.ds(dst, 1)], sem)

        def zero_tile(i):
            return pltpu.make_async_copy(zero_ref, xs_ref.at[pl.ds(pl.multiple_of(i * tm, tm), tm)], sem)

        for e in range(n_experts):
            lax.fori_loop(lo_ref[e], hi_ref[e], lambda r, c: (zero_row(r).start(), c)[1], 0)
            lax.fori_loop(lo_ref[e], hi_ref[e], lambda r, c: (zero_row(0).wait(), c)[1], 0)
        n_tiles = xs_ref.shape[0] // tm
        lax.fori_loop(nu_ref[0], n_tiles, lambda i, c: (zero_tile(i).start(), c)[1], 0)
        lax.fori_loop(nu_ref[0], n_tiles, lambda i, c: (zero_tile(0).wait(), c)[1], 0)


def _dispatch(x2, pos, pad_lo, pad_hi, n_used, n_rows, tm):
    t, d = x2.shape
    n_experts = pad_lo.shape[0]
    grid_spec = pltpu.PrefetchScalarGridSpec(
        num_scalar_prefetch=3,
        grid=(t // tm,),
        in_specs=[pl.BlockSpec((TOP_K * tm,), lambda i, *_: (i,), memory_space=pltpu.SMEM),
                  pl.BlockSpec((tm, d), lambda i, *_: (i, 0))],
        out_specs=pl.BlockSpec(memory_space=pl.ANY),
        scratch_shapes=[pltpu.VMEM((tm, d), F32), pltpu.SemaphoreType.DMA(())],
    )
    return pl.pallas_call(
        functools.partial(_dispatch_kernel, n_experts=n_experts),
        grid_spec=grid_spec,
        out_shape=jax.ShapeDtypeStruct((n_rows, d), F32),
        compiler_params=_cparams(("arbitrary",)),
        name="moe_dispatch",
    )(pad_lo, pad_hi, n_used, pos, x2)


def _expert_kernel(te_ref, nu_ref, x_ref, wg_ref, wu_ref, wd_ref, y_ref, acc_ref):
    used = pl.program_id(0) < nu_ref[0]

    @pl.when(used)
    def _():
        acc_ref[...] = jnp.zeros_like(acc_ref)
        _swiglu_acc(x_ref[...].astype(BF16), wg_ref, wu_ref, wd_ref, acc_ref, MXU_DIM)
        y_ref[...] = acc_ref[...]

    @pl.when(jnp.logical_not(used))
    def _():
        y_ref[...] = jnp.zeros_like(y_ref)


def _experts(xs, tile_expert, n_used, wg, wu, wd, li, tile):
    n_rows, d = xs.shape
    ffp = wg.shape[3]
    row = pl.BlockSpec((tile, d), lambda i, te, nu: (i, 0))
    grid_spec = pltpu.PrefetchScalarGridSpec(
        num_scalar_prefetch=2,
        grid=(n_rows // tile,),
        in_specs=[row,
                  pl.BlockSpec((None, None, d, ffp), lambda i, te, nu: (li, te[i], 0, 0)),
                  pl.BlockSpec((None, None, d, ffp), lambda i, te, nu: (li, te[i], 0, 0)),
                  pl.BlockSpec((None, None, ffp, d), lambda i, te, nu: (li, te[i], 0, 0))],
        out_specs=row,
        scratch_shapes=[pltpu.VMEM((tile, d), F32)],
    )
    return pl.pallas_call(
        _expert_kernel,
        grid_spec=grid_spec,
        out_shape=jax.ShapeDtypeStruct((n_rows, d), F32),
        compiler_params=_cparams(("arbitrary",)),
        name="moe_experts",
    )(tile_expert, n_used, xs, wg, wu, wd)


def _combine_kernel(pos_ref, x_ref, g_ref, ys_ref, lw_ref, lb_ref, o_ref, y0_ref, y1_ref, sem, *, alpha, n_experts):
    tm = x_ref.shape[0]

    def row_copy(src, buf, r):
        return pltpu.make_async_copy(ys_ref.at[pl.ds(src, 1)], buf.at[pl.ds(r, 1)], sem)

    def start(r, c):
        row_copy(pos_ref[2 * r], y0_ref, r).start(priority=0)
        row_copy(pos_ref[2 * r + 1], y1_ref, r).start(priority=1)
        return c

    lax.fori_loop(0, tm, start, 0, unroll=DMA_UNROLL)
    for buf in (y0_ref, y1_ref):
        pltpu.make_async_copy(ys_ref.at[pl.ds(0, tm)], buf, sem).wait()
    g = g_ref[...]
    lane = lax.broadcasted_iota(jnp.int32, g.shape, 1)
    w0 = jnp.sum(jnp.where(lane == n_experts + 2, g, 0.0), axis=1, keepdims=True)
    w1 = jnp.sum(jnp.where(lane == n_experts + 3, g, 0.0), axis=1, keepdims=True)
    f = w0 * y0_ref[...] + w1 * y1_ref[...]
    o_ref[...] = _layer_norm_rows(alpha * x_ref[...] + f, lw_ref[...], lb_ref[...])


def _combine(x2, gates, pos, ys, ln_w, ln_b, layer, n_experts, alpha, tm):
    t, d = x2.shape
    row = pl.BlockSpec((tm, d), lambda i: (i, 0))
    return pl.pallas_call(
        functools.partial(_combine_kernel, alpha=alpha, n_experts=n_experts),
        grid=(t // tm,),
        in_specs=[pl.BlockSpec((TOP_K * tm,), lambda i: (i,), memory_space=pltpu.SMEM), row,
                  pl.BlockSpec((tm, LANES), lambda i: (i, 0)), pl.BlockSpec(memory_space=pl.ANY),
                  _layer_of(ln_w.shape, layer), _layer_of(ln_b.shape, layer)],
        out_specs=row,
        out_shape=jax.ShapeDtypeStruct((t, d), F32),
        scratch_shapes=[pltpu.VMEM((tm, d), F32), pltpu.VMEM((tm, d), F32), pltpu.SemaphoreType.DMA(())],
        compiler_params=_cparams(("arbitrary",)),
        name="moe_combine_ln",
    )(pos, x2, gates, ys, ln_w, ln_b)


def _moe(x2, gates, wg, wu, wd, ln_w, ln_b, li, layer, alpha, tm):
    t, d = x2.shape
    n_experts = wg.shape[1]
    sel = gates[:, n_experts:n_experts + TOP_K].astype(jnp.int32)
    pos, tile_expert, n_used, pad_lo, pad_hi = _moe_plan(sel, n_experts, tm)
    xs = _dispatch(x2, pos, pad_lo, pad_hi, n_used, tile_expert.shape[0] * tm, tm)
    ys = _experts(xs, tile_expert, n_used, wg, wu, wd, li, tm)
    return _combine(x2, gates, pos, ys, ln_w, ln_b, layer, n_experts, alpha, tm)


def kernel(x, w_in, ssm_lam_re, ssm_lam_im, ssm_log_dt, ssm_b_re, ssm_b_im, ssm_c_re, ssm_c_im, ssm_d, ssm_glu_w, ssm_glu_b, ssm_norm_w, ret_gn_w, ret_gn_b, sb_norm_w, w_out, ln_mix_w, ln_mix_b, ffn_w_gate, ffn_w_up, ffn_w_down, moe_router, moe_w_gate, moe_w_up, moe_w_down, ln_ffn_w, ln_ffn_b):
    batch, seq, d = x.shape
    depth = w_in.shape[0]
    alpha = (2.0 * depth) ** 0.25
    ssm_w = ssm_d.shape[1]
    ret_w = ret_gn_w.shape[1]
    sb_w = sb_norm_w.shape[1]
    t = batch * seq
    tm = min(ROW_TILE, t)
    vec = lambda a: a[:, None, :]

    w_in_bf, w_out_bf = w_in.astype(BF16), w_out.astype(BF16)
    glu_w_bf = ssm_glu_w.astype(BF16)
    ffn_g = _pad_to(ffn_w_gate, 2, MXU_DIM).astype(BF16)
    ffn_u = _pad_to(ffn_w_up, 2, MXU_DIM).astype(BF16)
    ffn_d = _pad_to(ffn_w_down, 1, MXU_DIM).astype(BF16)
    moe_g = _pad_to(moe_w_gate, 3, MXU_DIM).astype(BF16)
    moe_u = _pad_to(moe_w_up, 3, MXU_DIM).astype(BF16)
    moe_d = _pad_to(moe_w_down, 2, MXU_DIM).astype(BF16)
    n_experts = moe_router.shape[2]
    router_hl = jnp.concatenate(_split_bf16(_pad_to(moe_router, 2, LANES)), axis=2)
    table_in = _s5_table_inputs(ssm_lam_re, ssm_lam_im, ssm_log_dt, ssm_b_re, ssm_b_im, ssm_c_re, ssm_c_im, ssm_d)
    s5_mt, s5_wt, s5_vt = _s5_tables(*table_in[:7])
    lam_pos, lam_sup = table_in[7:]
    ret_tables = _retention_tables(seq, ret_w // RET_HEAD_DIM)
    s5_rows = min(S5_ROWS, seq // S5_CHUNK)

    h = x.reshape(t, d)
    for layer in range(depth):
        a, rq, rk, rv, rg, sq, sk, sv = _proj(h, w_in_bf, layer, ssm_w, ret_w, sb_w, tm)
        xp = _s5_states(a, s5_wt, lam_pos, lam_sup, layer, batch, s5_rows)
        y_ssm = _s5_outputs(a, s5_mt, xp, s5_vt, glu_w_bf, vec(ssm_glu_b), vec(ssm_norm_w), layer, s5_rows)
        y_ret = _retention_mixer(rq, rk, rv, rg, ret_tables, vec(ret_gn_w), vec(ret_gn_b), layer, batch, seq)
        y_sb = _sb_mixer(sq, sk, sv, vec(sb_norm_w), layer, batch, seq)
        li = layer // 2
        if layer % 2 == 0:
            h = _mix(h, y_ssm, y_ret, y_sb, w_out_bf, vec(ln_mix_w), vec(ln_mix_b), layer, alpha, tm)
            h = _ffn(h, ffn_g, ffn_u, ffn_d, vec(ln_ffn_w), vec(ln_ffn_b), li, layer, alpha, tm)
        else:
            h, gates = _mix(h, y_ssm, y_ret, y_sb, w_out_bf, vec(ln_mix_w), vec(ln_mix_b), layer, alpha, tm,
                            router=(router_hl, li, n_experts))
            h = _moe(h, gates, moe_g, moe_u, moe_d, vec(ln_ffn_w), vec(ln_ffn_b), li, layer, alpha, tm)
    return h.reshape(batch, seq, d)
```

```python
import functools
import math

import jax
import jax.numpy as jnp
from jax import lax
from jax.experimental import pallas as pl
from jax.experimental.pallas import tpu as pltpu

F32 = jnp.float32
BF16 = jnp.bfloat16

SSM_GROUP_CH = 16
SSM_STATE = 64
RET_HEAD_DIM = 64
SB_HEAD_DIM = 64
RET_CHUNK = 128
ROPE_BASE = 10000.0
TOP_K = 2
LN_EPS = 1e-5
LOG2_E = math.log2(math.e)

LANES = 128
SUBLANES = 8
MXU_DIM = 256
VMEM_LIMIT_BYTES = 56 * 1024 * 1024

ROW_TILE = 512
S5_CHUNK = 16
S5_PAIR_CH = 2 * SSM_GROUP_CH
S5_SUPER = 16
S5_ROWS = 256
RET_STEPS = 4
SB_TILE = 256
SB_EXIT_LOG = 110.0
DMA_UNROLL = 8


def _cparams(sem, vmem=VMEM_LIMIT_BYTES):
    return pltpu.CompilerParams(dimension_semantics=sem, vmem_limit_bytes=vmem)


def _resident(shape):
    nd = len(shape)
    return pl.BlockSpec(shape, lambda *_: (0,) * nd)


def _layer_of(shape, layer, single_buffer=False):
    nd = len(shape)
    mode = pl.Buffered(1) if single_buffer else None
    return pl.BlockSpec((None,) + tuple(shape[1:]), lambda *_: (layer,) + (0,) * (nd - 1), pipeline_mode=mode)


def _layer_norm_rows(h, w, b):
    mu = jnp.mean(h, axis=-1, keepdims=True)
    d = h - mu
    var = jnp.mean(d * d, axis=-1, keepdims=True)
    return d * lax.rsqrt(var + LN_EPS) * w + b


def _split_bf16(v):
    hi = v.astype(BF16)
    lo = (v - hi.astype(F32)).astype(BF16)
    return hi, lo


def _dot_nt(a, b):
    return lax.dot_general(a, b, (((1,), (1,)), ((), ())), preferred_element_type=F32)


def _sigmoid(v):
    return 1.0 / (1.0 + jnp.exp(-v))


def _row_permutation(groups, per_group):
    n = groups * per_group
    src = jnp.arange(n)
    dst = (src % per_group) * groups + src // per_group
    return (dst[:, None] == jnp.arange(n)[None, :]).astype(BF16).T


def _proj_kernel(x_ref, w_ref, perm_ref, a_ref, rq_ref, rk_ref, rv_ref, rg_ref,
                 sq_ref, sk_ref, sv_ref, *, ssm_w, ret_w, sb_w):
    xb = x_ref[...].astype(BF16)
    col = [0]

    def mm(width):
        c0 = col[0]
        col[0] = c0 + width
        return jnp.dot(xb, w_ref[:, c0:c0 + width], preferred_element_type=F32)

    u = mm(ssm_w).astype(BF16)
    up = jnp.dot(perm_ref[...], u, preferred_element_type=F32).astype(BF16)
    nch = u.shape[0] // S5_CHUNK
    for k in range(ssm_w // S5_PAIR_CH):
        a_ref[k] = jnp.concatenate(
            [up[s * nch:(s + 1) * nch, k * S5_PAIR_CH:(k + 1) * S5_PAIR_CH] for s in range(S5_CHUNK)], axis=1)
    rq_ref[...] = mm(ret_w).astype(BF16)
    rk_ref[...] = mm(ret_w).astype(BF16)
    rv_ref[...] = mm(ret_w).astype(BF16)
    rg_ref[...] = mm(ret_w).astype(BF16)
    sq_ref[...] = mm(sb_w).astype(BF16)
    sk_ref[...] = mm(sb_w).astype(BF16)
    sv_ref[...] = mm(sb_w).astype(BF16)


def _proj(x2, w_bf, layer, ssm_w, ret_w, sb_w, tm):
    t, d = x2.shape
    pairs = ssm_w // S5_PAIR_CH
    nch = tm // S5_CHUNK
    perm = _row_permutation(nch, S5_CHUNK)
    row = lambda w: pl.BlockSpec((tm, w), lambda i: (i, 0))
    out_shape = (
        jax.ShapeDtypeStruct((pairs, t // S5_CHUNK, S5_CHUNK * S5_PAIR_CH), BF16),
        jax.ShapeDtypeStruct((t, ret_w), BF16), jax.ShapeDtypeStruct((t, ret_w), BF16),
        jax.ShapeDtypeStruct((t, ret_w), BF16), jax.ShapeDtypeStruct((t, ret_w), BF16),
        jax.ShapeDtypeStruct((t, sb_w), BF16), jax.ShapeDtypeStruct((t, sb_w), BF16),
        jax.ShapeDtypeStruct((t, sb_w), BF16),
    )
    out_specs = (pl.BlockSpec((pairs, nch, S5_CHUNK * S5_PAIR_CH), lambda i: (0, i, 0)),
                 row(ret_w), row(ret_w), row(ret_w), row(ret_w), row(sb_w), row(sb_w), row(sb_w))
    return pl.pallas_call(
        functools.partial(_proj_kernel, ssm_w=ssm_w, ret_w=ret_w, sb_w=sb_w),
        grid=(t // tm,),
        in_specs=[row(d), _layer_of(w_bf.shape, layer), _resident(perm.shape)],
        out_specs=out_specs,
        out_shape=out_shape,
        compiler_params=_cparams(("parallel",)),
        name="proj",
    )(x2, w_bf, perm)


def _s5_table_inputs(lam_re, lam_im, log_dt, b_re, b_im, c_re, c_im, d_skip):
    nl, g, p = lam_re.shape
    hch = b_re.shape[-1]
    pairs = g // 2
    lc = S5_CHUNK
    dt = jnp.exp(log_dt)[..., None]
    a_re, a_im = lam_re * dt, lam_im * dt

    def lam_pow(n):
        n = n[None, :, None, None]
        mag = jnp.exp(n * a_re[:, None])
        return mag * jnp.cos(n * a_im[:, None]), mag * jnp.sin(n * a_im[:, None])

    lb_re, lb_im = lam_pow(jnp.ones((1,), F32))
    lb_re, lb_im = lb_re[:, 0], lb_im[:, 0]
    den = lam_re * lam_re + lam_im * lam_im
    z_re = ((lb_re - 1.0) * lam_re + lb_im * lam_im) / den
    z_im = (lb_im * lam_re - (lb_re - 1.0) * lam_im) / den
    bb_re = z_re[..., None] * b_re - z_im[..., None] * b_im
    bb_im = z_re[..., None] * b_im + z_im[..., None] * b_re
    eye2 = jnp.eye(2, dtype=F32)

    def b_blocks(v):
        v = v.reshape(nl, pairs, 2, p, hch)
        return jnp.einsum('lkgph,gf->lkghfp', v, eye2).reshape(nl, pairs, 2 * hch, 2 * p)

    def c_blocks(v):
        v = v.reshape(nl, pairs, 2, hch, p)
        return jnp.einsum('lkghp,gf->lkghfp', v, eye2).reshape(nl, pairs, 2 * hch, 2 * p)

    pw_re, pw_im = lam_pow(jnp.arange(lc + 1, dtype=F32))
    pad = (-(lc + 1)) % SUBLANES

    def pair_rows(v):
        v = v.reshape(nl, lc + 1, pairs, 2 * p).transpose(0, 2, 1, 3)
        return jnp.pad(v, ((0, 0), (0, 0), (0, pad), (0, 0)))

    dd = d_skip.reshape(nl, pairs, 2 * hch)[..., None] * jnp.eye(2 * hch, lc * 2 * hch, dtype=F32)

    def pair_lanes(re, im):
        k = re.shape[1]
        v = jnp.stack([re.reshape(nl, k, pairs, 2 * p), im.reshape(nl, k, pairs, 2 * p)], axis=3)
        return v.reshape(nl, k, pairs * 4 * p)

    lam_pos = pair_lanes(*lam_pow(lc * jnp.arange(S5_SUPER, dtype=F32)))
    lam_sup = pair_lanes(*lam_pow(jnp.full((1,), float(lc * S5_SUPER), F32)))
    return (b_blocks(bb_re), b_blocks(bb_im), c_blocks(c_re), c_blocks(c_im), pair_rows(pw_re), pair_rows(pw_im),
            dd, lam_pos, lam_sup)


def _s5_table_kernel(bbr_ref, bbi_ref, ctr_ref, cti_ref, pwr_ref, pwi_ref, dd_ref, mt_ref, wt_ref, vt_ref):
    lc = S5_CHUNK
    bbr, bbi, ctr, cti = bbr_ref[...], bbi_ref[...], ctr_ref[...], cti_ref[...]
    ch = bbr.shape[0]
    pwr = [pwr_ref[l:l + 1, :] for l in range(lc + 1)]
    pwi = [pwi_ref[l:l + 1, :] for l in range(lc + 1)]
    clr = [ctr * pwr[l] - cti * pwi[l] for l in range(lc + 1)]
    cli = [ctr * pwi[l] + cti * pwr[l] for l in range(lc + 1)]

    def dot3(a, b):
        ah, al = _split_bf16(a)
        bh, bl = _split_bf16(b)
        return _dot_nt(ah, bh) + _dot_nt(al, bh) + _dot_nt(ah, bl)

    r0 = (dot3(bbr, jnp.concatenate(clr[:lc], axis=0)) - dot3(bbi, jnp.concatenate(cli[:lc], axis=0))
          + dd_ref[...])
    lane = lax.broadcasted_iota(jnp.int32, r0.shape, 1)
    for s in range(lc):
        blk = r0 if s == 0 else jnp.where(lane >= s * ch, pltpu.roll(r0, s * ch, axis=1), 0.0)
        mt_ref[s * ch:(s + 1) * ch, :] = blk.astype(BF16)
        l = lc - 1 - s
        wt_ref[s * ch:(s + 1) * ch, :] = jnp.concatenate(
            [bbr * pwr[l] - bbi * pwi[l], bbr * pwi[l] + bbi * pwr[l]], axis=1).astype(BF16)
    vt_ref[...] = jnp.concatenate([jnp.concatenate(clr[1:], axis=0), -jnp.concatenate(cli[1:], axis=0)],
                                  axis=1).astype(BF16)


def _s5_tables(bbr, bbi, ctr, cti, pwr, pwi, dd):
    nl, pairs, ch, sdim = bbr.shape
    kdim = S5_CHUNK * ch
    blk = lambda a: pl.BlockSpec((None, None) + a.shape[2:], lambda l, k: (l, k, 0, 0))
    out = lambda w: pl.BlockSpec((None, None, kdim, w), lambda l, k: (l, k, 0, 0))
    return pl.pallas_call(
        _s5_table_kernel,
        grid=(nl, pairs),
        in_specs=[blk(bbr), blk(bbi), blk(ctr), blk(cti), blk(pwr), blk(pwi), blk(dd)],
        out_specs=(out(kdim), out(2 * sdim), out(2 * sdim)),
        out_shape=(jax.ShapeDtypeStruct((nl, pairs, kdim, kdim), BF16),
                   jax.ShapeDtypeStruct((nl, pairs, kdim, 2 * sdim), BF16),
                   jax.ShapeDtypeStruct((nl, pairs, kdim, 2 * sdim), BF16)),
        compiler_params=_cparams(("parallel", "parallel")),
        name="s5_tables",
    )(bbr, bbi, ctr, cti, pwr, pwi, dd)


def _cmul(l, x):
    out = []
    for c0 in range(0, x.shape[1], 2 * LANES):
        lr, li = l[:, c0:c0 + LANES], l[:, c0 + LANES:c0 + 2 * LANES]
        xr, xi = x[:, c0:c0 + LANES], x[:, c0 + LANES:c0 + 2 * LANES]
        out += [lr * xr - li * xi, lr * xi + li * xr]
    return jnp.concatenate(out, axis=1)


def _s5_state_kernel(a_ref, wt_ref, lpos_ref, lsup_ref, perm_ref, permt_ref, xp_ref,
                     carry_ref, sin_ref, loc_ref, sup_ref, *, pairs, rows):
    @pl.when(pl.program_id(1) == 0)
    def _():
        carry_ref[...] = jnp.zeros_like(carry_ref)

    ns = rows // S5_SUPER
    blk = 2 * LANES
    perm = perm_ref[...]
    for k in range(pairs):
        ap = jnp.dot(perm, a_ref[k], preferred_element_type=F32).astype(BF16)
        sin_ref[:, k * blk:(k + 1) * blk] = jnp.dot(ap, wt_ref[k], preferred_element_type=F32)

    lam_chunk = lpos_ref[1:2, :]
    x = jnp.zeros((ns, sin_ref.shape[1]), F32)
    for j in range(S5_SUPER):
        loc_ref[j * ns:(j + 1) * ns, :] = x
        x = _cmul(lam_chunk, x) + sin_ref[j * ns:(j + 1) * ns, :]
    carry = carry_ref[...]
    lam_sup = lsup_ref[...]
    for m in range(ns):
        sup_ref[m:m + 1, :] = carry
        carry = _cmul(lam_sup, carry) + x[m:m + 1, :]
    carry_ref[...] = carry
    sup = sup_ref[...]
    for j in range(S5_SUPER):
        loc_ref[j * ns:(j + 1) * ns, :] += _cmul(lpos_ref[j:j + 1, :], sup)
    xp_ref[...] = jnp.dot(permt_ref[...], loc_ref[...].astype(BF16), preferred_element_type=F32).astype(BF16)


def _s5_states(a, wt, lam_pos, lam_sup, layer, batch, rows):
    pairs, n, kdim = a.shape
    per_b = n // batch // rows
    width = lam_pos.shape[2]
    ns = rows // S5_SUPER
    perm = _row_permutation(ns, S5_SUPER)
    return pl.pallas_call(
        functools.partial(_s5_state_kernel, pairs=pairs, rows=rows),
        grid=(batch, per_b),
        in_specs=[pl.BlockSpec((pairs, rows, kdim), lambda b, i: (0, b * per_b + i, 0)),
                  _layer_of(wt.shape, layer), _layer_of(lam_pos.shape, layer), _layer_of(lam_sup.shape, layer),
                  _resident(perm.shape), _resident(perm.shape)],
        out_specs=pl.BlockSpec((rows, width), lambda b, i: (b * per_b + i, 0)),
        out_shape=jax.ShapeDtypeStruct((n, width), BF16),
        scratch_shapes=[pltpu.VMEM((1, width), F32), pltpu.VMEM((rows, width), F32),
                        pltpu.VMEM((rows, width), F32), pltpu.VMEM((ns, width), F32)],
        compiler_params=_cparams(("arbitrary", "arbitrary")),
        name="s5_states",
    )(a, wt, lam_pos, lam_sup, perm, perm.T)


def _s5_out_kernel(a_ref, mt_ref, xp_ref, vt_ref, gw_ref, gb_ref, nw_ref, gm_ref, permt_ref, o_ref, *, pairs):
    rows = a_ref.shape[1]
    ch = S5_PAIR_CH
    sdim = vt_ref.shape[2]
    ys = [jnp.dot(a_ref[k], mt_ref[k], preferred_element_type=F32)
          + _dot_nt(xp_ref[:, k * sdim:(k + 1) * sdim], vt_ref[k]) for k in range(pairs)]
    outs = []
    for t in range(S5_CHUNK):
        y = jnp.concatenate([ys[k][:, t * ch:(t + 1) * ch] for k in range(pairs)], axis=1)
        g = y * (0.5 * (1.0 + jnp.tanh(math.sqrt(2.0 / math.pi) * (y + 0.044715 * (y * y * y)))))
        z = jnp.dot(g.astype(BF16), gw_ref[...], preferred_element_type=F32) + gb_ref[...]
        out = g * _sigmoid(z)
        hi, lo = _split_bf16(out * out)
        ms = (jnp.dot(hi, gm_ref[...], preferred_element_type=F32)
              + jnp.dot(lo, gm_ref[...], preferred_element_type=F32))
        outs.append((out * lax.rsqrt(ms + LN_EPS) * nw_ref[...]).astype(BF16))
    gch = permt_ref.shape[0] // S5_CHUNK
    for gi in range(rows // gch):
        stack = jnp.concatenate([outs[t][gi * gch:(gi + 1) * gch, :] for t in range(S5_CHUNK)], axis=0)
        o_ref[gi * gch * S5_CHUNK:(gi + 1) * gch * S5_CHUNK, :] = jnp.dot(
            permt_ref[...], stack, preferred_element_type=F32).astype(BF16)


def _s5_outputs(a, mt, xp, vt, glu_w, glu_b, norm_w, layer, rows):
    pairs, n, kdim = a.shape
    w = glu_w.shape[1]
    gch = min(ROW_TILE // S5_CHUNK, rows)
    groups = jnp.arange(w) // SSM_GROUP_CH
    gmean = jnp.where(groups[:, None] == groups[None, :], 1.0 / SSM_GROUP_CH, 0.0).astype(BF16)
    permt = _row_permutation(gch, S5_CHUNK).T
    return pl.pallas_call(
        functools.partial(_s5_out_kernel, pairs=pairs),
        grid=(n // rows,),
        in_specs=[pl.BlockSpec((pairs, rows, kdim), lambda i: (0, i, 0)), _layer_of(mt.shape, layer),
                  pl.BlockSpec((rows, xp.shape[1]), lambda i: (i, 0)), _layer_of(vt.shape, layer),
                  _layer_of(glu_w.shape, layer), _layer_of(glu_b.shape, layer), _layer_of(norm_w.shape, layer),
                  _resident(gmean.shape), _resident(permt.shape)],
        out_specs=pl.BlockSpec((rows * S5_CHUNK, w), lambda i: (i, 0)),
        out_shape=jax.ShapeDtypeStruct((n * S5_CHUNK, w), BF16),
        compiler_params=_cparams(("parallel",)),
        name="s5_outputs",
    )(a, mt, xp, vt, glu_w, glu_b, norm_w, gmean, permt)


def _retention_kernel(q_ref, k_ref, v_ref, g_ref, cos_ref, sin_ref, qdec_ref, kdec_ref, dmask_ref,
                      cdec_ref, gnw_ref, gnb_ref, gm_ref, bmask_ref, o_ref, state_ref, *, heads, chunk):
    width = q_ref.shape[1]
    c = chunk
    half_w = MXU_DIM
    halves = width // half_w
    heads_per_half = heads // halves
    dh = width // heads

    @pl.when(pl.program_id(1) == 0)
    def _():
        state_ref[...] = jnp.zeros_like(state_ref)

    lane = lax.broadcasted_iota(jnp.int32, (c, width), 1)
    first_half = (lane % dh) < (dh // 2)
    lane_h = lax.broadcasted_iota(jnp.int32, (c, half_w), 1) // dh

    def group_mean(a):
        hi, lo = _split_bf16(a)
        parts = []
        for hf in range(halves):
            sl = slice(hf * half_w, (hf + 1) * half_w)
            parts.append(jnp.dot(hi[:, sl], gm_ref[...], preferred_element_type=F32)
                         + jnp.dot(lo[:, sl], gm_ref[...], preferred_element_type=F32))
        return jnp.concatenate(parts, axis=1)

    for step in range(q_ref.shape[0] // c):
        rs = slice(step * c, (step + 1) * c)

        reps = width // cos_ref.shape[1]
        cos_w = jnp.concatenate([cos_ref[rs, :]] * reps, axis=1)
        sin_w = jnp.concatenate([sin_ref[rs, :]] * reps, axis=1)

        def rope(x):
            swapped = jnp.where(first_half, pltpu.roll(x, width - dh // 2, axis=1), pltpu.roll(x, dh // 2, axis=1))
            return x * cos_w + swapped * sin_w

        q = rope(q_ref[rs, :].astype(F32))
        k = rope(k_ref[rs, :].astype(F32)) * (dh ** -0.5)
        v = v_ref[rs, :]
        qb = q.astype(BF16)
        kb = k.astype(BF16)
        qd = (q * qdec_ref[...]).astype(BF16)
        kd = k * kdec_ref[...]

        outs = []
        for hf in range(halves):
            sl = slice(hf * half_w, (hf + 1) * half_w)
            qh, kh, vh = qb[:, sl], kb[:, sl], v[:, sl]
            scores, vstack = [], []
            for j in range(heads_per_half):
                sel = lane_h == j
                s = _dot_nt(jnp.where(sel, qh, jnp.zeros_like(qh)), kh)
                scores.append((s * dmask_ref[hf * heads_per_half + j]).astype(BF16))
                vstack.append(jnp.where(sel, vh, jnp.zeros_like(vh)))
            inner = jnp.dot(jnp.concatenate(scores, axis=1), jnp.concatenate(vstack, axis=0),
                            preferred_element_type=F32)
            st = state_ref[hf]
            cross = jnp.dot(qd[:, sl], st.astype(BF16), preferred_element_type=F32)
            outs.append(inner + cross)
            upd = jnp.dot(kd[:, sl].T.astype(BF16), vh, preferred_element_type=F32)
            state_ref[hf] = st * cdec_ref[:, sl] + upd * bmask_ref[...]
        o = jnp.concatenate(outs, axis=1)

        d = o - group_mean(o)
        var = group_mean(d * d)
        on = d * lax.rsqrt(var + LN_EPS) * gnw_ref[...] + gnb_ref[...]
        g = g_ref[rs, :].astype(F32)
        o_ref[rs, :] = (g * _sigmoid(g) * on).astype(BF16)


def _retention_tables(seq, heads):
    dh = RET_HEAD_DIM
    c = RET_CHUNK
    half = dh // 2
    pos = jnp.arange(seq, dtype=F32)
    inv = ROPE_BASE ** (-jnp.arange(half, dtype=F32) / half)
    ang = pos[:, None] * inv[None, :]
    cos_t = jnp.tile(jnp.concatenate([jnp.cos(ang), jnp.cos(ang)], axis=1), (1, LANES // dh))
    sin_t = jnp.tile(jnp.concatenate([-jnp.sin(ang), jnp.sin(ang)], axis=1), (1, LANES // dh))
    log_gamma = jnp.log(1.0 - 2.0 ** (-5.0 - jnp.arange(heads, dtype=F32)))
    idx = jnp.arange(c, dtype=F32)
    rel = idx[:, None] - idx[None, :]
    dmask = jnp.where(rel >= 0, jnp.exp(log_gamma[:, None, None] * jnp.maximum(rel, 0.0)), 0.0)
    per_lane = lambda a: jnp.repeat(a, dh, axis=-1)
    qdec = per_lane(jnp.exp(log_gamma[None, :] * (idx[:, None] + 1.0)))
    kdec = per_lane(jnp.exp(log_gamma[None, :] * (c - 1.0 - idx[:, None])))
    cdec = per_lane(jnp.exp(log_gamma * c)[None, :])
    hid = jnp.arange(MXU_DIM) // dh
    same = hid[:, None] == hid[None, :]
    return cos_t, sin_t, qdec, kdec, dmask, cdec, jnp.where(same, 1.0 / dh, 0.0).astype(BF16), same.astype(F32)


def _retention_mixer(rq, rk, rv, rg, tables, gn_w, gn_b, layer, batch, seq):
    t, width = rq.shape
    heads = width // RET_HEAD_DIM
    c = RET_CHUNK
    rows = c * min(RET_STEPS, seq // c)
    n = seq // rows
    cos_t, sin_t, qdec, kdec, dmask, cdec, gmean, bmask = tables
    row = pl.BlockSpec((rows, width), lambda b, i: (b * n + i, 0))
    tab = pl.BlockSpec((rows, LANES), lambda b, i: (i, 0))
    halves = width // MXU_DIM
    return pl.pallas_call(
        functools.partial(_retention_kernel, heads=heads, chunk=c),
        grid=(batch, n),
        in_specs=[row, row, row, row, tab, tab, _resident((c, width)), _resident((c, width)),
                  _resident((heads, c, c)), _resident((1, width)), _layer_of(gn_w.shape, layer),
                  _layer_of(gn_b.shape, layer), _resident((MXU_DIM, MXU_DIM)), _resident((MXU_DIM, MXU_DIM))],
        out_specs=row,
        out_shape=jax.ShapeDtypeStruct((t, width), BF16),
        scratch_shapes=[pltpu.VMEM((halves, MXU_DIM, MXU_DIM), F32)],
        compiler_params=_cparams(("arbitrary", "arbitrary")),
        name="retention",
    )(rq, rk, rv, rg, cos_t, sin_t, qdec, kdec, dmask, cdec, gn_w, gn_b, gmean, bmask)


def _sb_kernel(q_ref, k_ref, v_ref, nw_ref, tri_ref, gm_ref, o_ref, acc_ref, run_ref, *, heads):
    tq, width = q_ref.shape
    tk = tq
    dh = width // heads
    i = pl.program_id(1)
    q = q_ref[...] * (dh ** -0.5)
    lane_q = lax.broadcasted_iota(jnp.int32, (tq, width), 1) // dh
    lane_k = lax.broadcasted_iota(jnp.int32, (tk, width), 1) // dh
    qms = [jnp.where(lane_q == h, q, jnp.zeros_like(q)) for h in range(heads)]
    causal = lax.broadcasted_iota(jnp.int32, (tq, tk), 1) < lax.broadcasted_iota(jnp.int32, (tq, tk), 0)

    def tile(j, masked):
        start = pl.multiple_of(j * tk, tk)
        kb = k_ref[pl.ds(start, tk), :]
        vb = v_ref[pl.ds(start, tk), :]
        weights, vstack, low = [], [], None
        for h in range(heads):
            z = _dot_nt(qms[h], kb)
            sp = jnp.maximum(z, 0.0) + jnp.log(1.0 + jnp.exp2(jnp.abs(z) * (-LOG2_E)))
            sp_sum = jnp.where(causal, sp, 0.0) if masked else sp
            hi, lo = _split_bf16(sp_sum)
            later = jnp.dot(jnp.concatenate([hi, lo], axis=1), tri_ref[...], preferred_element_type=F32)
            run = run_ref[h]
            a = jnp.exp2((z - sp - later - run) * LOG2_E)
            if masked:
                a = jnp.where(causal, a, 0.0)
            weights.append(a.astype(BF16))
            vstack.append(jnp.where(lane_k == h, vb, jnp.zeros_like(vb)))
            run = run + jnp.sum(sp_sum, axis=1, keepdims=True)
            run_ref[h] = run
            m = jnp.min(run)
            low = m if low is None else jnp.minimum(low, m)
        acc_ref[...] += jnp.dot(jnp.concatenate(weights, axis=1), jnp.concatenate(vstack, axis=0),
                                preferred_element_type=F32)
        return low

    acc_ref[...] = jnp.zeros_like(acc_ref)
    run_ref[...] = jnp.zeros_like(run_ref)
    low = tile(i, True)

    def cond(carry):
        j, low = carry
        return jnp.logical_and(j >= 0, low < SB_EXIT_LOG)

    def body(carry):
        j, _ = carry
        return j - 1, tile(j, False)

    lax.while_loop(cond, body, (i - 1, low))
    o = acc_ref[...]
    hi, lo = _split_bf16(o * o)
    ms = (jnp.dot(hi, gm_ref[...], preferred_element_type=F32) + jnp.dot(lo, gm_ref[...], preferred_element_type=F32))
    o_ref[...] = (o * lax.rsqrt(ms + LN_EPS) * nw_ref[...]).astype(BF16)


def _sb_mixer(sq, sk, sv, norm_w, layer, batch, seq):
    t, width = sq.shape
    heads = width // SB_HEAD_DIM
    tq = min(SB_TILE, seq)
    nq = seq // tq
    idx = jnp.arange(tq)
    tri = (idx[:, None] > idx[None, :]).astype(BF16)
    tri2 = jnp.concatenate([tri, tri], axis=0)
    hid = jnp.arange(width) // SB_HEAD_DIM
    gmean = jnp.where(hid[:, None] == hid[None, :], 1.0 / SB_HEAD_DIM, 0.0).astype(BF16)
    whole = pl.BlockSpec((None, seq, width), lambda b, i: (b, 0, 0))
    row = pl.BlockSpec((tq, width), lambda b, i: (b * nq + i, 0))
    return pl.pallas_call(
        functools.partial(_sb_kernel, heads=heads),
        grid=(batch, nq),
        in_specs=[row, whole, whole, _layer_of(norm_w.shape, layer), _resident((2 * tq, tq)),
                  _resident((width, width))],
        out_specs=row,
        out_shape=jax.ShapeDtypeStruct((t, width), BF16),
        scratch_shapes=[pltpu.VMEM((tq, width), F32), pltpu.VMEM((heads, tq, 1), F32)],
        compiler_params=_cparams(("parallel", "arbitrary")),
        name="stick_breaking",
    )(sq, sk.reshape(batch, seq, width), sv.reshape(batch, seq, width), norm_w, tri2, gmean)


def _top2_gates(x, w_hl, n_experts):
    xh, xl = _split_bf16(x)
    both = jnp.dot(xh, w_hl, preferred_element_type=F32)
    logits = both[:, :LANES] + both[:, LANES:] + jnp.dot(xl, w_hl[:, :LANES], preferred_element_type=F32)
    lane = lax.broadcasted_iota(jnp.int32, logits.shape, 1).astype(F32)
    neg = -1e30
    logits = jnp.where(lane < n_experts, logits, neg)
    m1 = jnp.max(logits, axis=1, keepdims=True)
    i1 = jnp.min(jnp.where(logits == m1, lane, float(LANES)), axis=1, keepdims=True)
    rest = jnp.where(lane == i1, neg, logits)
    m2 = jnp.max(rest, axis=1, keepdims=True)
    i2 = jnp.min(jnp.where(rest == m2, lane, float(LANES)), axis=1, keepdims=True)
    e2 = jnp.exp(m2 - m1)
    w1 = 1.0 / (1.0 + e2)
    w2 = e2 * w1
    return (jnp.where(lane == i1, w1, 0.0) + jnp.where(lane == i2, w2, 0.0)
            + jnp.where(lane == n_experts, i1, 0.0) + jnp.where(lane == n_experts + 1, i2, 0.0)
            + jnp.where(lane == n_experts + 2, w1, 0.0) + jnp.where(lane == n_experts + 3, w2, 0.0))


def _mix_kernel(x_ref, ya_ref, yb_ref, yc_ref, w_ref, lw_ref, lb_ref, *rest, alpha, n_experts):
    wa = ya_ref.shape[1]
    wb = yb_ref.shape[1]
    mix = (jnp.dot(ya_ref[...], w_ref[0:wa, :], preferred_element_type=F32)
           + jnp.dot(yb_ref[...], w_ref[wa:wa + wb, :], preferred_element_type=F32)
           + jnp.dot(yc_ref[...], w_ref[wa + wb:, :], preferred_element_type=F32))
    out = _layer_norm_rows(alpha * x_ref[...] + mix, lw_ref[...], lb_ref[...])
    if n_experts:
        whl_ref, o_ref, g_ref = rest
        g_ref[...] = _top2_gates(out, whl_ref[...], n_experts)
    else:
        (o_ref,) = rest
    o_ref[...] = out


def _mix(x2, ya, yb, yc, w_bf, ln_w, ln_b, layer, alpha, tm, router=None):
    t, d = x2.shape
    row = lambda w: pl.BlockSpec((tm, w), lambda i: (i, 0))
    in_specs = [row(d), row(ya.shape[1]), row(yb.shape[1]), row(yc.shape[1]), _layer_of(w_bf.shape, layer),
                _layer_of(ln_w.shape, layer), _layer_of(ln_b.shape, layer)]
    args = [x2, ya, yb, yc, w_bf, ln_w, ln_b]
    out_specs, out_shape, n_experts = row(d), jax.ShapeDtypeStruct((t, d), F32), 0
    if router is not None:
        w_hl, li, n_experts = router
        in_specs += [_layer_of(w_hl.shape, li)]
        args += [w_hl]
        out_specs = (out_specs, row(LANES))
        out_shape = (out_shape, jax.ShapeDtypeStruct((t, LANES), F32))
    return pl.pallas_call(
        functools.partial(_mix_kernel, alpha=alpha, n_experts=n_experts),
        grid=(t // tm,),
        in_specs=in_specs,
        out_specs=out_specs,
        out_shape=out_shape,
        compiler_params=_cparams(("parallel",)),
        name="mix_ln",
    )(*args)


def _swiglu_acc(xb, wg_ref, wu_ref, wd_ref, acc_ref, chunk, scale=None):
    ff = wg_ref.shape[-1]
    for c0 in range(0, ff, chunk):
        g = jnp.dot(xb, wg_ref[:, c0:c0 + chunk], preferred_element_type=F32)
        u = jnp.dot(xb, wu_ref[:, c0:c0 + chunk], preferred_element_type=F32)
        h = g * _sigmoid(g) * u
        if scale is not None:
            h = h * scale
        acc_ref[...] += jnp.dot(h.astype(BF16), wd_ref[c0:c0 + chunk, :], preferred_element_type=F32)


def _mix_ffn_kernel(x_ref, ya_ref, yb_ref, yc_ref, wo_ref, mw_ref, mb_ref, wg_ref, wu_ref, wd_ref, lw_ref, lb_ref,
                    o_ref, x1_ref, acc_ref, *, alpha):
    wa = ya_ref.shape[1]
    wb = yb_ref.shape[1]
    mix = (jnp.dot(ya_ref[...], wo_ref[0:wa, :], preferred_element_type=F32)
           + jnp.dot(yb_ref[...], wo_ref[wa:wa + wb, :], preferred_element_type=F32)
           + jnp.dot(yc_ref[...], wo_ref[wa + wb:, :], preferred_element_type=F32))
    x1_ref[...] = _layer_norm_rows(alpha * x_ref[...] + mix, mw_ref[...], mb_ref[...])
    acc_ref[...] = jnp.zeros_like(acc_ref)
    _swiglu_acc(x1_ref[...].astype(BF16), wg_ref, wu_ref, wd_ref, acc_ref, MXU_DIM)
    o_ref[...] = _layer_norm_rows(alpha * x1_ref[...] + acc_ref[...], lw_ref[...], lb_ref[...])


def _pad_to(a, axis, mult):
    size = a.shape[axis]
    pad = (-size) % mult
    if pad == 0:
        return a
    widths = [(0, 0)] * a.ndim
    widths[axis] = (0, pad)
    return jnp.pad(a, widths)


def _mix_ffn(x2, ya, yb, yc, w_out, mix_w, mix_b, wg, wu, wd, ln_w, ln_b, li, layer, alpha, tm):
    t, d = x2.shape
    row = lambda w: pl.BlockSpec((tm, w), lambda i: (i, 0))
    return pl.pallas_call(
        functools.partial(_mix_ffn_kernel, alpha=alpha),
        grid=(t // tm,),
        in_specs=[row(d), row(ya.shape[1]), row(yb.shape[1]), row(yc.shape[1]),
                  _layer_of(w_out.shape, layer, True), _layer_of(mix_w.shape, layer), _layer_of(mix_b.shape, layer),
                  _layer_of(wg.shape, li, True), _layer_of(wu.shape, li, True), _layer_of(wd.shape, li, True),
                  _layer_of(ln_w.shape, layer), _layer_of(ln_b.shape, layer)],
        out_specs=row(d),
        out_shape=jax.ShapeDtypeStruct((t, d), F32),
        scratch_shapes=[pltpu.VMEM((tm, d), F32), pltpu.VMEM((tm, d), F32)],
        compiler_params=_cparams(("parallel",)),
        name="mix_ffn_ln",
    )(x2, ya, yb, yc, w_out, mix_w, mix_b, wg, wu, wd, ln_w, ln_b)


def _moe_plan(sel, n_experts, tile):
    e_flat = sel.reshape(-1)
    onehot = (e_flat[None, :] == jnp.arange(n_experts)[:, None]).astype(jnp.int32)
    csum = jnp.cumsum(onehot, axis=1)
    counts = csum[:, -1]
    rank = jnp.sum(onehot * csum, axis=0) - 1
    padded = (counts + tile - 1) // tile * tile
    ends = jnp.cumsum(padded)
    starts = ends - padded
    pos = jnp.sum(onehot * starts[:, None], axis=0) + rank
    n_tiles = e_flat.shape[0] // tile + n_experts
    first_row = jnp.arange(n_tiles, dtype=jnp.int32)[:, None] * tile
    tile_expert = jnp.minimum(jnp.sum((first_row >= ends[None, :]).astype(jnp.int32), axis=1), n_experts - 1)
    return (pos.astype(jnp.int32), tile_expert.astype(jnp.int32), (ends[-1:] // tile).astype(jnp.int32),
            (starts + counts).astype(jnp.int32), ends.astype(jnp.int32))


def _dispatch_kernel(lo_ref, hi_ref, nu_ref, pos_ref, x_ref, xs_ref, zero_ref, sem, *, n_experts):
    tm = x_ref.shape[0]

    def row_copy(r, dst):
        return pltpu.make_async_copy(x_ref.at[pl.ds(r, 1)], xs_ref.at[pl.ds(dst, 1)], sem)

    def start(r, c):
        row_copy(r, pos_ref[2 * r]).start(priority=0)
        row_copy(r, pos_ref[2 * r + 1]).start(priority=1)
        return c

    lax.fori_loop(0, tm, start, 0, unroll=DMA_UNROLL)
    for _ in range(TOP_K):
        pltpu.make_async_copy(x_ref, xs_ref.at[pl.ds(0, tm)], sem).wait()

    @pl.when(pl.program_id(0) == 0)
    def _():
        zero_ref[...] = jnp.zeros_like(zero_ref)

        def zero_row(dst):
            return pltpu.make_async_copy(zero_ref.at[pl.ds(0, 1)], xs_ref.at[pl.ds(dst, 1)], sem)

        def zero_tile(i):
            return pltpu.make_async_copy(zero_ref, xs_ref.at[pl.ds(pl.multiple_of(i * tm, tm), tm)], sem)

        for e in range(n_experts):
            lax.fori_loop(lo_ref[e], hi_ref[e], lambda r, c: (zero_row(r).start(), c)[1], 0)
            lax.fori_loop(lo_ref[e], hi_ref[e], lambda r, c: (zero_row(0).wait(), c)[1], 0)
        n_tiles = xs_ref.shape[0] // tm
        lax.fori_loop(nu_ref[0], n_tiles, lambda i, c: (zero_tile(i).start(), c)[1], 0)
        lax.fori_loop(nu_ref[0], n_tiles, lambda i, c: (zero_tile(0).wait(), c)[1], 0)


def _dispatch(x2, pos, pad_lo, pad_hi, n_used, n_rows, tm):
    t, d = x2.shape
    n_experts = pad_lo.shape[0]
    grid_spec = pltpu.PrefetchScalarGridSpec(
        num_scalar_prefetch=3,
        grid=(t // tm,),
        in_specs=[pl.BlockSpec((TOP_K * tm,), lambda i, *_: (i,), memory_space=pltpu.SMEM),
                  pl.BlockSpec((tm, d), lambda i, *_: (i, 0))],
        out_specs=pl.BlockSpec(memory_space=pl.ANY),
        scratch_shapes=[pltpu.VMEM((tm, d), F32), pltpu.SemaphoreType.DMA(())],
    )
    return pl.pallas_call(
        functools.partial(_dispatch_kernel, n_experts=n_experts),
        grid_spec=grid_spec,
        out_shape=jax.ShapeDtypeStruct((n_rows, d), F32),
        compiler_params=_cparams(("arbitrary",)),
        name="moe_dispatch",
    )(pad_lo, pad_hi, n_used, pos, x2)


def _expert_kernel(te_ref, nu_ref, x_ref, wg_ref, wu_ref, wd_ref, y_ref, acc_ref):
    used = pl.program_id(0) < nu_ref[0]

    @pl.when(used)
    def _():
        acc_ref[...] = jnp.zeros_like(acc_ref)
        _swiglu_acc(x_ref[...].astype(BF16), wg_ref, wu_ref, wd_ref, acc_ref, MXU_DIM)
        y_ref[...] = acc_ref[...]

    @pl.when(jnp.logical_not(used))
    def _():
        y_ref[...] = jnp.zeros_like(y_ref)


def _experts(xs, tile_expert, n_used, wg, wu, wd, li, tile):
    n_rows, d = xs.shape
    ffp = wg.shape[3]
    row = pl.BlockSpec((tile, d), lambda i, te, nu: (i, 0))
    grid_spec = pltpu.PrefetchScalarGridSpec(
        num_scalar_prefetch=2,
        grid=(n_rows // tile,),
        in_specs=[row,
                  pl.BlockSpec((None, None, d, ffp), lambda i, te, nu: (li, te[i], 0, 0)),
                  pl.BlockSpec((None, None, d, ffp), lambda i, te, nu: (li, te[i], 0, 0)),
                  pl.BlockSpec((None, None, ffp, d), lambda i, te, nu: (li, te[i], 0, 0))],
        out_specs=row,
        scratch_shapes=[pltpu.VMEM((tile, d), F32)],
    )
    return pl.pallas_call(
        _expert_kernel,
        grid_spec=grid_spec,
        out_shape=jax.ShapeDtypeStruct((n_rows, d), F32),
        compiler_params=_cparams(("arbitrary",)),
        name="moe_experts",
    )(tile_expert, n_used, xs, wg, wu, wd)


def _combine_kernel(pos_ref, x_ref, g_ref, ys_ref, lw_ref, lb_ref, o_ref, y0_ref, y1_ref, sem, *, alpha, n_experts):
    tm = x_ref.shape[0]

    def row_copy(src, buf, r):
        return pltpu.make_async_copy(ys_ref.at[pl.ds(src, 1)], buf.at[pl.ds(r, 1)], sem)

    def start(r, c):
        row_copy(pos_ref[2 * r], y0_ref, r).start(priority=0)
        row_copy(pos_ref[2 * r + 1], y1_ref, r).start(priority=1)
        return c

    lax.fori_loop(0, tm, start, 0, unroll=DMA_UNROLL)
    for buf in (y0_ref, y1_ref):
        pltpu.make_async_copy(ys_ref.at[pl.ds(0, tm)], buf, sem).wait()
    g = g_ref[...]
    lane = lax.broadcasted_iota(jnp.int32, g.shape, 1)
    w0 = jnp.sum(jnp.where(lane == n_experts + 2, g, 0.0), axis=1, keepdims=True)
    w1 = jnp.sum(jnp.where(lane == n_experts + 3, g, 0.0), axis=1, keepdims=True)
    f = w0 * y0_ref[...] + w1 * y1_ref[...]
    o_ref[...] = _layer_norm_rows(alpha * x_ref[...] + f, lw_ref[...], lb_ref[...])


def _combine(x2, gates, pos, ys, ln_w, ln_b, layer, n_experts, alpha, tm):
    t, d = x2.shape
    row = pl.BlockSpec((tm, d), lambda i: (i, 0))
    return pl.pallas_call(
        functools.partial(_combine_kernel, alpha=alpha, n_experts=n_experts),
        grid=(t // tm,),
        in_specs=[pl.BlockSpec((TOP_K * tm,), lambda i: (i,), memory_space=pltpu.SMEM), row,
                  pl.BlockSpec((tm, LANES), lambda i: (i, 0)), pl.BlockSpec(memory_space=pl.ANY),
                  _layer_of(ln_w.shape, layer), _layer_of(ln_b.shape, layer)],
        out_specs=row,
        out_shape=jax.ShapeDtypeStruct((t, d), F32),
        scratch_shapes=[pltpu.VMEM((tm, d), F32), pltpu.VMEM((tm, d), F32), pltpu.SemaphoreType.DMA(())],
        compiler_params=_cparams(("arbitrary",)),
        name="moe_combine_ln",
    )(pos, x2, gates, ys, ln_w, ln_b)


def _moe(x2, gates, wg, wu, wd, ln_w, ln_b, li, layer, alpha, tm):
    t, d = x2.shape
    n_experts = wg.shape[1]
    sel = gates[:, n_experts:n_experts + TOP_K].astype(jnp.int32)
    pos, tile_expert, n_used, pad_lo, pad_hi = _moe_plan(sel, n_experts, tm)
    xs = _dispatch(x2, pos, pad_lo, pad_hi, n_used, tile_expert.shape[0] * tm, tm)
    ys = _experts(xs, tile_expert, n_used, wg, wu, wd, li, tm)
    return _combine(x2, gates, pos, ys, ln_w, ln_b, layer, n_experts, alpha, tm)


def kernel(x, w_in, ssm_lam_re, ssm_lam_im, ssm_log_dt, ssm_b_re, ssm_b_im, ssm_c_re, ssm_c_im, ssm_d, ssm_glu_w, ssm_glu_b, ssm_norm_w, ret_gn_w, ret_gn_b, sb_norm_w, w_out, ln_mix_w, ln_mix_b, ffn_w_gate, ffn_w_up, ffn_w_down, moe_router, moe_w_gate, moe_w_up, moe_w_down, ln_ffn_w, ln_ffn_b):
    batch, seq, d = x.shape
    depth = w_in.shape[0]
    alpha = (2.0 * depth) ** 0.25
    ssm_w = ssm_d.shape[1]
    ret_w = ret_gn_w.shape[1]
    sb_w = sb_norm_w.shape[1]
    t = batch * seq
    tm = min(ROW_TILE, t)
    vec = lambda a: a[:, None, :]

    w_in_bf, w_out_bf = w_in.astype(BF16), w_out.astype(BF16)
    glu_w_bf = ssm_glu_w.astype(BF16)
    ffn_g = _pad_to(ffn_w_gate, 2, MXU_DIM).astype(BF16)
    ffn_u = _pad_to(ffn_w_up, 2, MXU_DIM).astype(BF16)
    ffn_d = _pad_to(ffn_w_down, 1, MXU_DIM).astype(BF16)
    moe_g = _pad_to(moe_w_gate, 3, MXU_DIM).astype(BF16)
    moe_u = _pad_to(moe_w_up, 3, MXU_DIM).astype(BF16)
    moe_d = _pad_to(moe_w_down, 2, MXU_DIM).astype(BF16)
    n_experts = moe_router.shape[2]
    router_hl = jnp.concatenate(_split_bf16(_pad_to(moe_router, 2, LANES)), axis=2)
    table_in = _s5_table_inputs(ssm_lam_re, ssm_lam_im, ssm_log_dt, ssm_b_re, ssm_b_im, ssm_c_re, ssm_c_im, ssm_d)
    s5_mt, s5_wt, s5_vt = _s5_tables(*table_in[:7])
    lam_pos, lam_sup = table_in[7:]
    ret_tables = _retention_tables(seq, ret_w // RET_HEAD_DIM)
    s5_rows = min(S5_ROWS, seq // S5_CHUNK)

    h = x.reshape(t, d)
    for layer in range(depth):
        a, rq, rk, rv, rg, sq, sk, sv = _proj(h, w_in_bf, layer, ssm_w, ret_w, sb_w, tm)
        xp = _s5_states(a, s5_wt, lam_pos, lam_sup, layer, batch, s5_rows)
        y_ssm = _s5_outputs(a, s5_mt, xp, s5_vt, glu_w_bf, vec(ssm_glu_b), vec(ssm_norm_w), layer, s5_rows)
        y_ret = _retention_mixer(rq, rk, rv, rg, ret_tables, vec(ret_gn_w), vec(ret_gn_b), layer, batch, seq)
        y_sb = _sb_mixer(sq, sk, sv, vec(sb_norm_w), layer, batch, seq)
        li = layer // 2
        if layer % 2 == 0:
            h = _mix_ffn(h, y_ssm, y_ret, y_sb, w_out_bf, vec(ln_mix_w), vec(ln_mix_b), ffn_g, ffn_u, ffn_d,
                         vec(ln_ffn_w), vec(ln_ffn_b), li, layer, alpha, tm)
        else:
            h, gates = _mix(h, y_ssm, y_ret, y_sb, w_out_bf, vec(ln_mix_w), vec(ln_mix_b), layer, alpha, tm,
                            router=(router_hl, li, n_experts))
            h = _moe(h, gates, moe_g, moe_u, moe_d, vec(ln_ffn_w), vec(ln_ffn_b), li, layer, alpha, tm)
    return h.reshape(batch, seq, d)
```

```python
import functools
import math

import jax
import jax.numpy as jnp
from jax import lax
from jax.experimental import pallas as pl
from jax.experimental.pallas import tpu as pltpu

F32 = jnp.float32
BF16 = jnp.bfloat16

SSM_GROUP_CH = 16
SSM_STATE = 64
RET_HEAD_DIM = 64
SB_HEAD_DIM = 64
RET_CHUNK = 128
ROPE_BASE = 10000.0
TOP_K = 2
LN_EPS = 1e-5
LOG2_E = math.log2(math.e)

LANES = 128
SUBLANES = 8
MXU_DIM = 256
VMEM_LIMIT_BYTES = 56 * 1024 * 1024

ROW_TILE = 512
FFN_ROW_TILE = 1024
S5_CHUNK = 16
S5_PAIR_CH = 2 * SSM_GROUP_CH
S5_SUPER = 16
S5_ROWS = 256
RET_STEPS = 4
SB_TILE = 256
SB_EXIT_LOG = 110.0
DMA_UNROLL = 8


def _cparams(sem, vmem=VMEM_LIMIT_BYTES):
    return pltpu.CompilerParams(dimension_semantics=sem, vmem_limit_bytes=vmem)


def _resident(shape):
    nd = len(shape)
    return pl.BlockSpec(shape, lambda *_: (0,) * nd)


def _layer_of(shape, layer, single_buffer=False):
    nd = len(shape)
    mode = pl.Buffered(1) if single_buffer else None
    return pl.BlockSpec((None,) + tuple(shape[1:]), lambda *_: (layer,) + (0,) * (nd - 1), pipeline_mode=mode)


def _layer_norm_rows(h, w, b):
    mu = jnp.mean(h, axis=-1, keepdims=True)
    d = h - mu
    var = jnp.mean(d * d, axis=-1, keepdims=True)
    return d * lax.rsqrt(var + LN_EPS) * w + b


def _split_bf16(v):
    hi = v.astype(BF16)
    lo = (v - hi.astype(F32)).astype(BF16)
    return hi, lo


def _dot_nt(a, b):
    return lax.dot_general(a, b, (((1,), (1,)), ((), ())), preferred_element_type=F32)


def _sigmoid(v):
    return 1.0 / (1.0 + jnp.exp(-v))


def _row_permutation(groups, per_group):
    n = groups * per_group
    src = jnp.arange(n)
    dst = (src % per_group) * groups + src // per_group
    return (dst[:, None] == jnp.arange(n)[None, :]).astype(BF16).T


def _proj_kernel(x_ref, w_ref, perm_ref, a_ref, rq_ref, rk_ref, rv_ref, rg_ref,
                 sq_ref, sk_ref, sv_ref, *, ssm_w, ret_w, sb_w):
    xb = x_ref[...].astype(BF16)
    col = [0]

    def mm(width):
        c0 = col[0]
        col[0] = c0 + width
        return jnp.dot(xb, w_ref[:, c0:c0 + width], preferred_element_type=F32)

    u = mm(ssm_w).astype(BF16)
    up = jnp.dot(perm_ref[...], u, preferred_element_type=F32).astype(BF16)
    nch = u.shape[0] // S5_CHUNK
    for k in range(ssm_w // S5_PAIR_CH):
        a_ref[k] = jnp.concatenate(
            [up[s * nch:(s + 1) * nch, k * S5_PAIR_CH:(k + 1) * S5_PAIR_CH] for s in range(S5_CHUNK)], axis=1)
    rq_ref[...] = mm(ret_w).astype(BF16)
    rk_ref[...] = mm(ret_w).astype(BF16)
    rv_ref[...] = mm(ret_w).astype(BF16)
    rg_ref[...] = mm(ret_w).astype(BF16)
    sq_ref[...] = mm(sb_w).astype(BF16)
    sk_ref[...] = mm(sb_w).astype(BF16)
    sv_ref[...] = mm(sb_w).astype(BF16)


def _proj(x2, w_bf, layer, ssm_w, ret_w, sb_w, tm):
    t, d = x2.shape
    pairs = ssm_w // S5_PAIR_CH
    nch = tm // S5_CHUNK
    perm = _row_permutation(nch, S5_CHUNK)
    row = lambda w: pl.BlockSpec((tm, w), lambda i: (i, 0))
    out_shape = (
        jax.ShapeDtypeStruct((pairs, t // S5_CHUNK, S5_CHUNK * S5_PAIR_CH), BF16),
        jax.ShapeDtypeStruct((t, ret_w), BF16), jax.ShapeDtypeStruct((t, ret_w), BF16),
        jax.ShapeDtypeStruct((t, ret_w), BF16), jax.ShapeDtypeStruct((t, ret_w), BF16),
        jax.ShapeDtypeStruct((t, sb_w), BF16), jax.ShapeDtypeStruct((t, sb_w), BF16),
        jax.ShapeDtypeStruct((t, sb_w), BF16),
    )
    out_specs = (pl.BlockSpec((pairs, nch, S5_CHUNK * S5_PAIR_CH), lambda i: (0, i, 0)),
                 row(ret_w), row(ret_w), row(ret_w), row(ret_w), row(sb_w), row(sb_w), row(sb_w))
    return pl.pallas_call(
        functools.partial(_proj_kernel, ssm_w=ssm_w, ret_w=ret_w, sb_w=sb_w),
        grid=(t // tm,),
        in_specs=[row(d), _layer_of(w_bf.shape, layer), _resident(perm.shape)],
        out_specs=out_specs,
        out_shape=out_shape,
        compiler_params=_cparams(("parallel",)),
        name="proj",
    )(x2, w_bf, perm)


def _s5_table_inputs(lam_re, lam_im, log_dt, b_re, b_im, c_re, c_im, d_skip):
    nl, g, p = lam_re.shape
    hch = b_re.shape[-1]
    pairs = g // 2
    lc = S5_CHUNK
    dt = jnp.exp(log_dt)[..., None]
    a_re, a_im = lam_re * dt, lam_im * dt

    def lam_pow(n):
        n = n[None, :, None, None]
        mag = jnp.exp(n * a_re[:, None])
        return mag * jnp.cos(n * a_im[:, None]), mag * jnp.sin(n * a_im[:, None])

    lb_re, lb_im = lam_pow(jnp.ones((1,), F32))
    lb_re, lb_im = lb_re[:, 0], lb_im[:, 0]
    den = lam_re * lam_re + lam_im * lam_im
    z_re = ((lb_re - 1.0) * lam_re + lb_im * lam_im) / den
    z_im = (lb_im * lam_re - (lb_re - 1.0) * lam_im) / den
    bb_re = z_re[..., None] * b_re - z_im[..., None] * b_im
    bb_im = z_re[..., None] * b_im + z_im[..., None] * b_re
    eye2 = jnp.eye(2, dtype=F32)

    def b_blocks(v):
        v = v.reshape(nl, pairs, 2, p, hch)
        return jnp.einsum('lkgph,gf->lkghfp', v, eye2).reshape(nl, pairs, 2 * hch, 2 * p)

    def c_blocks(v):
        v = v.reshape(nl, pairs, 2, hch, p)
        return jnp.einsum('lkghp,gf->lkghfp', v, eye2).reshape(nl, pairs, 2 * hch, 2 * p)

    pw_re, pw_im = lam_pow(jnp.arange(lc + 1, dtype=F32))
    pad = (-(lc + 1)) % SUBLANES

    def pair_rows(v):
        v = v.reshape(nl, lc + 1, pairs, 2 * p).transpose(0, 2, 1, 3)
        return jnp.pad(v, ((0, 0), (0, 0), (0, pad), (0, 0)))

    dd = d_skip.reshape(nl, pairs, 2 * hch)[..., None] * jnp.eye(2 * hch, lc * 2 * hch, dtype=F32)

    def pair_lanes(re, im):
        k = re.shape[1]
        v = jnp.stack([re.reshape(nl, k, pairs, 2 * p), im.reshape(nl, k, pairs, 2 * p)], axis=3)
        return v.reshape(nl, k, pairs * 4 * p)

    lam_pos = pair_lanes(*lam_pow(lc * jnp.arange(S5_SUPER, dtype=F32)))
    lam_sup = pair_lanes(*lam_pow(jnp.full((1,), float(lc * S5_SUPER), F32)))
    return (b_blocks(bb_re), b_blocks(bb_im), c_blocks(c_re), c_blocks(c_im), pair_rows(pw_re), pair_rows(pw_im),
            dd, lam_pos, lam_sup)


def _s5_table_kernel(bbr_ref, bbi_ref, ctr_ref, cti_ref, pwr_ref, pwi_ref, dd_ref, mt_ref, wt_ref, vt_ref):
    lc = S5_CHUNK
    bbr, bbi, ctr, cti = bbr_ref[...], bbi_ref[...], ctr_ref[...], cti_ref[...]
    ch = bbr.shape[0]
    pwr = [pwr_ref[l:l + 1, :] for l in range(lc + 1)]
    pwi = [pwi_ref[l:l + 1, :] for l in range(lc + 1)]
    clr = [ctr * pwr[l] - cti * pwi[l] for l in range(lc + 1)]
    cli = [ctr * pwi[l] + cti * pwr[l] for l in range(lc + 1)]

    def dot3(a, b):
        ah, al = _split_bf16(a)
        bh, bl = _split_bf16(b)
        return _dot_nt(ah, bh) + _dot_nt(al, bh) + _dot_nt(ah, bl)

    r0 = (dot3(bbr, jnp.concatenate(clr[:lc], axis=0)) - dot3(bbi, jnp.concatenate(cli[:lc], axis=0))
          + dd_ref[...])
    lane = lax.broadcasted_iota(jnp.int32, r0.shape, 1)
    for s in range(lc):
        blk = r0 if s == 0 else jnp.where(lane >= s * ch, pltpu.roll(r0, s * ch, axis=1), 0.0)
        mt_ref[s * ch:(s + 1) * ch, :] = blk.astype(BF16)
        l = lc - 1 - s
        wt_ref[s * ch:(s + 1) * ch, :] = jnp.concatenate(
            [bbr * pwr[l] - bbi * pwi[l], bbr * pwi[l] + bbi * pwr[l]], axis=1).astype(BF16)
    vt_ref[...] = jnp.concatenate([jnp.concatenate(clr[1:], axis=0), -jnp.concatenate(cli[1:], axis=0)],
                                  axis=1).astype(BF16)


def _s5_tables(bbr, bbi, ctr, cti, pwr, pwi, dd):
    nl, pairs, ch, sdim = bbr.shape
    kdim = S5_CHUNK * ch
    blk = lambda a: pl.BlockSpec((None, None) + a.shape[2:], lambda l, k: (l, k, 0, 0))
    out = lambda w: pl.BlockSpec((None, None, kdim, w), lambda l, k: (l, k, 0, 0))
    return pl.pallas_call(
        _s5_table_kernel,
        grid=(nl, pairs),
        in_specs=[blk(bbr), blk(bbi), blk(ctr), blk(cti), blk(pwr), blk(pwi), blk(dd)],
        out_specs=(out(kdim), out(2 * sdim), out(2 * sdim)),
        out_shape=(jax.ShapeDtypeStruct((nl, pairs, kdim, kdim), BF16),
                   jax.ShapeDtypeStruct((nl, pairs, kdim, 2 * sdim), BF16),
                   jax.ShapeDtypeStruct((nl, pairs, kdim, 2 * sdim), BF16)),
        compiler_params=_cparams(("parallel", "parallel")),
        name="s5_tables",
    )(bbr, bbi, ctr, cti, pwr, pwi, dd)


def _cmul(l, x):
    out = []
    for c0 in range(0, x.shape[1], 2 * LANES):
        lr, li = l[:, c0:c0 + LANES], l[:, c0 + LANES:c0 + 2 * LANES]
        xr, xi = x[:, c0:c0 + LANES], x[:, c0 + LANES:c0 + 2 * LANES]
        out += [lr * xr - li * xi, lr * xi + li * xr]
    return jnp.concatenate(out, axis=1)


def _s5_state_kernel(a_ref, wt_ref, lpos_ref, lsup_ref, perm_ref, permt_ref, xp_ref,
                     carry_ref, sin_ref, loc_ref, sup_ref, *, pairs, rows):
    @pl.when(pl.program_id(1) == 0)
    def _():
        carry_ref[...] = jnp.zeros_like(carry_ref)

    ns = rows // S5_SUPER
    blk = 2 * LANES
    perm = perm_ref[...]
    for k in range(pairs):
        ap = jnp.dot(perm, a_ref[k], preferred_element_type=F32).astype(BF16)
        sin_ref[:, k * blk:(k + 1) * blk] = jnp.dot(ap, wt_ref[k], preferred_element_type=F32)

    lam_chunk = lpos_ref[1:2, :]
    x = jnp.zeros((ns, sin_ref.shape[1]), F32)
    for j in range(S5_SUPER):
        loc_ref[j * ns:(j + 1) * ns, :] = x
        x = _cmul(lam_chunk, x) + sin_ref[j * ns:(j + 1) * ns, :]
    carry = carry_ref[...]
    lam_sup = lsup_ref[...]
    for m in range(ns):
        sup_ref[m:m + 1, :] = carry
        carry = _cmul(lam_sup, carry) + x[m:m + 1, :]
    carry_ref[...] = carry
    sup = sup_ref[...]
    for j in range(S5_SUPER):
        loc_ref[j * ns:(j + 1) * ns, :] += _cmul(lpos_ref[j:j + 1, :], sup)
    xp_ref[...] = jnp.dot(permt_ref[...], loc_ref[...].astype(BF16), preferred_element_type=F32).astype(BF16)


def _s5_states(a, wt, lam_pos, lam_sup, layer, batch, rows):
    pairs, n, kdim = a.shape
    per_b = n // batch // rows
    width = lam_pos.shape[2]
    ns = rows // S5_SUPER
    perm = _row_permutation(ns, S5_SUPER)
    return pl.pallas_call(
        functools.partial(_s5_state_kernel, pairs=pairs, rows=rows),
        grid=(batch, per_b),
        in_specs=[pl.BlockSpec((pairs, rows, kdim), lambda b, i: (0, b * per_b + i, 0)),
                  _layer_of(wt.shape, layer), _layer_of(lam_pos.shape, layer), _layer_of(lam_sup.shape, layer),
                  _resident(perm.shape), _resident(perm.shape)],
        out_specs=pl.BlockSpec((rows, width), lambda b, i: (b * per_b + i, 0)),
        out_shape=jax.ShapeDtypeStruct((n, width), BF16),
        scratch_shapes=[pltpu.VMEM((1, width), F32), pltpu.VMEM((rows, width), F32),
                        pltpu.VMEM((rows, width), F32), pltpu.VMEM((ns, width), F32)],
        compiler_params=_cparams(("arbitrary", "arbitrary")),
        name="s5_states",
    )(a, wt, lam_pos, lam_sup, perm, perm.T)


def _s5_out_kernel(a_ref, mt_ref, xp_ref, vt_ref, gw_ref, gb_ref, nw_ref, gm_ref, permt_ref, o_ref, *, pairs):
    rows = a_ref.shape[1]
    ch = S5_PAIR_CH
    sdim = vt_ref.shape[2]
    ys = [jnp.dot(a_ref[k], mt_ref[k], preferred_element_type=F32)
          + _dot_nt(xp_ref[:, k * sdim:(k + 1) * sdim], vt_ref[k]) for k in range(pairs)]
    outs = []
    for t in range(S5_CHUNK):
        y = jnp.concatenate([ys[k][:, t * ch:(t + 1) * ch] for k in range(pairs)], axis=1)
        g = y * (0.5 * (1.0 + jnp.tanh(math.sqrt(2.0 / math.pi) * (y + 0.044715 * (y * y * y)))))
        z = jnp.dot(g.astype(BF16), gw_ref[...], preferred_element_type=F32) + gb_ref[...]
        out = g * _sigmoid(z)
        hi, lo = _split_bf16(out * out)
        ms = (jnp.dot(hi, gm_ref[...], preferred_element_type=F32)
              + jnp.dot(lo, gm_ref[...], preferred_element_type=F32))
        outs.append((out * lax.rsqrt(ms + LN_EPS) * nw_ref[...]).astype(BF16))
    gch = permt_ref.shape[0] // S5_CHUNK
    for gi in range(rows // gch):
        stack = jnp.concatenate([outs[t][gi * gch:(gi + 1) * gch, :] for t in range(S5_CHUNK)], axis=0)
        o_ref[gi * gch * S5_CHUNK:(gi + 1) * gch * S5_CHUNK, :] = jnp.dot(
            permt_ref[...], stack, preferred_element_type=F32).astype(BF16)


def _s5_outputs(a, mt, xp, vt, glu_w, glu_b, norm_w, layer, rows):
    pairs, n, kdim = a.shape
    w = glu_w.shape[1]
    gch = min(ROW_TILE // S5_CHUNK, rows)
    groups = jnp.arange(w) // SSM_GROUP_CH
    gmean = jnp.where(groups[:, None] == groups[None, :], 1.0 / SSM_GROUP_CH, 0.0).astype(BF16)
    permt = _row_permutation(gch, S5_CHUNK).T
    return pl.pallas_call(
        functools.partial(_s5_out_kernel, pairs=pairs),
        grid=(n // rows,),
        in_specs=[pl.BlockSpec((pairs, rows, kdim), lambda i: (0, i, 0)), _layer_of(mt.shape, layer),
                  pl.BlockSpec((rows, xp.shape[1]), lambda i: (i, 0)), _layer_of(vt.shape, layer),
                  _layer_of(glu_w.shape, layer), _layer_of(glu_b.shape, layer), _layer_of(norm_w.shape, layer),
                  _resident(gmean.shape), _resident(permt.shape)],
        out_specs=pl.BlockSpec((rows * S5_CHUNK, w), lambda i: (i, 0)),
        out_shape=jax.ShapeDtypeStruct((n * S5_CHUNK, w), BF16),
        compiler_params=_cparams(("parallel",)),
        name="s5_outputs",
    )(a, mt, xp, vt, glu_w, glu_b, norm_w, gmean, permt)


def _retention_kernel(q_ref, k_ref, v_ref, g_ref, cos_ref, sin_ref, qdec_ref, kdec_ref, dmask_ref,
                      cdec_ref, gnw_ref, gnb_ref, gm_ref, bmask_ref, o_ref, state_ref, *, heads, chunk):
    width = q_ref.shape[1]
    c = chunk
    half_w = MXU_DIM
    halves = width // half_w
    heads_per_half = heads // halves
    dh = width // heads

    @pl.when(pl.program_id(1) == 0)
    def _():
        state_ref[...] = jnp.zeros_like(state_ref)

    lane = lax.broadcasted_iota(jnp.int32, (c, width), 1)
    first_half = (lane % dh) < (dh // 2)
    lane_h = lax.broadcasted_iota(jnp.int32, (c, half_w), 1) // dh

    def group_mean(a):
        hi, lo = _split_bf16(a)
        parts = []
        for hf in range(halves):
            sl = slice(hf * half_w, (hf + 1) * half_w)
            parts.append(jnp.dot(hi[:, sl], gm_ref[...], preferred_element_type=F32)
                         + jnp.dot(lo[:, sl], gm_ref[...], preferred_element_type=F32))
        return jnp.concatenate(parts, axis=1)

    for step in range(q_ref.shape[0] // c):
        rs = slice(step * c, (step + 1) * c)

        reps = width // cos_ref.shape[1]
        cos_w = jnp.concatenate([cos_ref[rs, :]] * reps, axis=1)
        sin_w = jnp.concatenate([sin_ref[rs, :]] * reps, axis=1)

        def rope(x):
            swapped = jnp.where(first_half, pltpu.roll(x, width - dh // 2, axis=1), pltpu.roll(x, dh // 2, axis=1))
            return x * cos_w + swapped * sin_w

        q = rope(q_ref[rs, :].astype(F32))
        k = rope(k_ref[rs, :].astype(F32)) * (dh ** -0.5)
        v = v_ref[rs, :]
        qb = q.astype(BF16)
        kb = k.astype(BF16)
        qd = (q * qdec_ref[...]).astype(BF16)
        kd = k * kdec_ref[...]

        outs = []
        for hf in range(halves):
            sl = slice(hf * half_w, (hf + 1) * half_w)
            qh, kh, vh = qb[:, sl], kb[:, sl], v[:, sl]
            scores, vstack = [], []
            for j in range(heads_per_half):
                sel = lane_h == j
                s = _dot_nt(jnp.where(sel, qh, jnp.zeros_like(qh)), kh)
                scores.append((s * dmask_ref[hf * heads_per_half + j]).astype(BF16))
                vstack.append(jnp.where(sel, vh, jnp.zeros_like(vh)))
            inner = jnp.dot(jnp.concatenate(scores, axis=1), jnp.concatenate(vstack, axis=0),
                            preferred_element_type=F32)
            st = state_ref[hf]
            cross = jnp.dot(qd[:, sl], st.astype(BF16), preferred_element_type=F32)
            outs.append(inner + cross)
            upd = jnp.dot(kd[:, sl].T.astype(BF16), vh, preferred_element_type=F32)
            state_ref[hf] = st * cdec_ref[:, sl] + upd * bmask_ref[...]
        o = jnp.concatenate(outs, axis=1)

        d = o - group_mean(o)
        var = group_mean(d * d)
        on = d * lax.rsqrt(var + LN_EPS) * gnw_ref[...] + gnb_ref[...]
        g = g_ref[rs, :].astype(F32)
        o_ref[rs, :] = (g * _sigmoid(g) * on).astype(BF16)


def _retention_tables(seq, heads):
    dh = RET_HEAD_DIM
    c = RET_CHUNK
    half = dh // 2
    pos = jnp.arange(seq, dtype=F32)
    inv = ROPE_BASE ** (-jnp.arange(half, dtype=F32) / half)
    ang = pos[:, None] * inv[None, :]
    cos_t = jnp.tile(jnp.concatenate([jnp.cos(ang), jnp.cos(ang)], axis=1), (1, LANES // dh))
    sin_t = jnp.tile(jnp.concatenate([-jnp.sin(ang), jnp.sin(ang)], axis=1), (1, LANES // dh))
    log_gamma = jnp.log(1.0 - 2.0 ** (-5.0 - jnp.arange(heads, dtype=F32)))
    idx = jnp.arange(c, dtype=F32)
    rel = idx[:, None] - idx[None, :]
    dmask = jnp.where(rel >= 0, jnp.exp(log_gamma[:, None, None] * jnp.maximum(rel, 0.0)), 0.0)
    per_lane = lambda a: jnp.repeat(a, dh, axis=-1)
    qdec = per_lane(jnp.exp(log_gamma[None, :] * (idx[:, None] + 1.0)))
    kdec = per_lane(jnp.exp(log_gamma[None, :] * (c - 1.0 - idx[:, None])))
    cdec = per_lane(jnp.exp(log_gamma * c)[None, :])
    hid = jnp.arange(MXU_DIM) // dh
    same = hid[:, None] == hid[None, :]
    return cos_t, sin_t, qdec, kdec, dmask, cdec, jnp.where(same, 1.0 / dh, 0.0).astype(BF16), same.astype(F32)


def _retention_mixer(rq, rk, rv, rg, tables, gn_w, gn_b, layer, batch, seq):
    t, width = rq.shape
    heads = width // RET_HEAD_DIM
    c = RET_CHUNK
    rows = c * min(RET_STEPS, seq // c)
    n = seq // rows
    cos_t, sin_t, qdec, kdec, dmask, cdec, gmean, bmask = tables
    row = pl.BlockSpec((rows, width), lambda b, i: (b * n + i, 0))
    tab = pl.BlockSpec((rows, LANES), lambda b, i: (i, 0))
    halves = width // MXU_DIM
    return pl.pallas_call(
        functools.partial(_retention_kernel, heads=heads, chunk=c),
        grid=(batch, n),
        in_specs=[row, row, row, row, tab, tab, _resident((c, width)), _resident((c, width)),
                  _resident((heads, c, c)), _resident((1, width)), _layer_of(gn_w.shape, layer),
                  _layer_of(gn_b.shape, layer), _resident((MXU_DIM, MXU_DIM)), _resident((MXU_DIM, MXU_DIM))],
        out_specs=row,
        out_shape=jax.ShapeDtypeStruct((t, width), BF16),
        scratch_shapes=[pltpu.VMEM((halves, MXU_DIM, MXU_DIM), F32)],
        compiler_params=_cparams(("arbitrary", "arbitrary")),
        name="retention",
    )(rq, rk, rv, rg, cos_t, sin_t, qdec, kdec, dmask, cdec, gn_w, gn_b, gmean, bmask)


def _sb_kernel(q_ref, k_ref, v_ref, nw_ref, tri_ref, gm_ref, o_ref, acc_ref, run_ref, *, heads):
    tq, width = q_ref.shape
    tk = tq
    dh = width // heads
    i = pl.program_id(1)
    q = q_ref[...] * (dh ** -0.5)
    lane_q = lax.broadcasted_iota(jnp.int32, (tq, width), 1) // dh
    lane_k = lax.broadcasted_iota(jnp.int32, (tk, width), 1) // dh
    qms = [jnp.where(lane_q == h, q, jnp.zeros_like(q)) for h in range(heads)]
    causal = lax.broadcasted_iota(jnp.int32, (tq, tk), 1) < lax.broadcasted_iota(jnp.int32, (tq, tk), 0)

    def tile(j, masked):
        start = pl.multiple_of(j * tk, tk)
        kb = k_ref[pl.ds(start, tk), :]
        vb = v_ref[pl.ds(start, tk), :]
        weights, vstack, low = [], [], None
        for h in range(heads):
            z = _dot_nt(qms[h], kb)
            sp = jnp.maximum(z, 0.0) + jnp.log(1.0 + jnp.exp2(jnp.abs(z) * (-LOG2_E)))
            sp_sum = jnp.where(causal, sp, 0.0) if masked else sp
            hi, lo = _split_bf16(sp_sum)
            later = jnp.dot(jnp.concatenate([hi, lo], axis=1), tri_ref[...], preferred_element_type=F32)
            run = run_ref[h]
            a = jnp.exp2((z - sp - later - run) * LOG2_E)
            if masked:
                a = jnp.where(causal, a, 0.0)
            weights.append(a.astype(BF16))
            vstack.append(jnp.where(lane_k == h, vb, jnp.zeros_like(vb)))
            run = run + jnp.sum(sp_sum, axis=1, keepdims=True)
            run_ref[h] = run
            m = jnp.min(run)
            low = m if low is None else jnp.minimum(low, m)
        acc_ref[...] += jnp.dot(jnp.concatenate(weights, axis=1), jnp.concatenate(vstack, axis=0),
                                preferred_element_type=F32)
        return low

    acc_ref[...] = jnp.zeros_like(acc_ref)
    run_ref[...] = jnp.zeros_like(run_ref)
    low = tile(i, True)

    def cond(carry):
        j, low = carry
        return jnp.logical_and(j >= 0, low < SB_EXIT_LOG)

    def body(carry):
        j, _ = carry
        return j - 1, tile(j, False)

    lax.while_loop(cond, body, (i - 1, low))
    o = acc_ref[...]
    hi, lo = _split_bf16(o * o)
    ms = (jnp.dot(hi, gm_ref[...], preferred_element_type=F32) + jnp.dot(lo, gm_ref[...], preferred_element_type=F32))
    o_ref[...] = (o * lax.rsqrt(ms + LN_EPS) * nw_ref[...]).astype(BF16)


def _sb_mixer(sq, sk, sv, norm_w, layer, batch, seq):
    t, width = sq.shape
    heads = width // SB_HEAD_DIM
    tq = min(SB_TILE, seq)
    nq = seq // tq
    idx = jnp.arange(tq)
    tri = (idx[:, None] > idx[None, :]).astype(BF16)
    tri2 = jnp.concatenate([tri, tri], axis=0)
    hid = jnp.arange(width) // SB_HEAD_DIM
    gmean = jnp.where(hid[:, None] == hid[None, :], 1.0 / SB_HEAD_DIM, 0.0).astype(BF16)
    whole = pl.BlockSpec((None, seq, width), lambda b, i: (b, 0, 0))
    row = pl.BlockSpec((tq, width), lambda b, i: (b * nq + i, 0))
    return pl.pallas_call(
        functools.partial(_sb_kernel, heads=heads),
        grid=(batch, nq),
        in_specs=[row, whole, whole, _layer_of(norm_w.shape, layer), _resident((2 * tq, tq)),
                  _resident((width, width))],
        out_specs=row,
        out_shape=jax.ShapeDtypeStruct((t, width), BF16),
        scratch_shapes=[pltpu.VMEM((tq, width), F32), pltpu.VMEM((heads, tq, 1), F32)],
        compiler_params=_cparams(("parallel", "arbitrary")),
        name="stick_breaking",
    )(sq, sk.reshape(batch, seq, width), sv.reshape(batch, seq, width), norm_w, tri2, gmean)


def _top2_gates(x, w_hl, n_experts):
    xh, xl = _split_bf16(x)
    both = jnp.dot(xh, w_hl, preferred_element_type=F32)
    logits = both[:, :LANES] + both[:, LANES:] + jnp.dot(xl, w_hl[:, :LANES], preferred_element_type=F32)
    lane = lax.broadcasted_iota(jnp.int32, logits.shape, 1).astype(F32)
    neg = -1e30
    logits = jnp.where(lane < n_experts, logits, neg)
    m1 = jnp.max(logits, axis=1, keepdims=True)
    i1 = jnp.min(jnp.where(logits == m1, lane, float(LANES)), axis=1, keepdims=True)
    rest = jnp.where(lane == i1, neg, logits)
    m2 = jnp.max(rest, axis=1, keepdims=True)
    i2 = jnp.min(jnp.where(rest == m2, lane, float(LANES)), axis=1, keepdims=True)
    e2 = jnp.exp(m2 - m1)
    w1 = 1.0 / (1.0 + e2)
    w2 = e2 * w1
    return (jnp.where(lane == i1, w1, 0.0) + jnp.where(lane == i2, w2, 0.0)
            + jnp.where(lane == n_experts, i1, 0.0) + jnp.where(lane == n_experts + 1, i2, 0.0)
            + jnp.where(lane == n_experts + 2, w1, 0.0) + jnp.where(lane == n_experts + 3, w2, 0.0))


def _mix_kernel(x_ref, ya_ref, yb_ref, yc_ref, w_ref, lw_ref, lb_ref, *rest, alpha, n_experts):
    wa = ya_ref.shape[1]
    wb = yb_ref.shape[1]
    mix = (jnp.dot(ya_ref[...], w_ref[0:wa, :], preferred_element_type=F32)
           + jnp.dot(yb_ref[...], w_ref[wa:wa + wb, :], preferred_element_type=F32)
           + jnp.dot(yc_ref[...], w_ref[wa + wb:, :], preferred_element_type=F32))
    out = _layer_norm_rows(alpha * x_ref[...] + mix, lw_ref[...], lb_ref[...])
    if n_experts:
        whl_ref, o_ref, g_ref = rest
        g_ref[...] = _top2_gates(out, whl_ref[...], n_experts)
    else:
        (o_ref,) = rest
    o_ref[...] = out


def _mix(x2, ya, yb, yc, w_bf, ln_w, ln_b, layer, alpha, tm, router=None):
    t, d = x2.shape
    row = lambda w: pl.BlockSpec((tm, w), lambda i: (i, 0))
    in_specs = [row(d), row(ya.shape[1]), row(yb.shape[1]), row(yc.shape[1]), _layer_of(w_bf.shape, layer),
                _layer_of(ln_w.shape, layer), _layer_of(ln_b.shape, layer)]
    args = [x2, ya, yb, yc, w_bf, ln_w, ln_b]
    out_specs, out_shape, n_experts = row(d), jax.ShapeDtypeStruct((t, d), F32), 0
    if router is not None:
        w_hl, li, n_experts = router
        in_specs += [_layer_of(w_hl.shape, li)]
        args += [w_hl]
        out_specs = (out_specs, row(LANES))
        out_shape = (out_shape, jax.ShapeDtypeStruct((t, LANES), F32))
    return pl.pallas_call(
        functools.partial(_mix_kernel, alpha=alpha, n_experts=n_experts),
        grid=(t // tm,),
        in_specs=in_specs,
        out_specs=out_specs,
        out_shape=out_shape,
        compiler_params=_cparams(("parallel",)),
        name="mix_ln",
    )(*args)


def _swiglu_acc(xb, wg_ref, wu_ref, wd_ref, acc_ref, chunk, scale=None):
    ff = wg_ref.shape[-1]
    for c0 in range(0, ff, chunk):
        g = jnp.dot(xb, wg_ref[:, c0:c0 + chunk], preferred_element_type=F32)
        u = jnp.dot(xb, wu_ref[:, c0:c0 + chunk], preferred_element_type=F32)
        h = g * _sigmoid(g) * u
        if scale is not None:
            h = h * scale
        acc_ref[...] += jnp.dot(h.astype(BF16), wd_ref[c0:c0 + chunk, :], preferred_element_type=F32)


def _mix_ffn_kernel(x_ref, ya_ref, yb_ref, yc_ref, wo_ref, mw_ref, mb_ref, wg_ref, wu_ref, wd_ref, lw_ref, lb_ref,
                    o_ref, x1_ref, acc_ref, *, alpha):
    wa = ya_ref.shape[1]
    wb = yb_ref.shape[1]
    mix = (jnp.dot(ya_ref[...], wo_ref[0:wa, :], preferred_element_type=F32)
           + jnp.dot(yb_ref[...], wo_ref[wa:wa + wb, :], preferred_element_type=F32)
           + jnp.dot(yc_ref[...], wo_ref[wa + wb:, :], preferred_element_type=F32))
    x1_ref[...] = _layer_norm_rows(alpha * x_ref[...] + mix, mw_ref[...], mb_ref[...])
    acc_ref[...] = jnp.zeros_like(acc_ref)
    _swiglu_acc(x1_ref[...].astype(BF16), wg_ref, wu_ref, wd_ref, acc_ref, MXU_DIM)
    o_ref[...] = _layer_norm_rows(alpha * x1_ref[...] + acc_ref[...], lw_ref[...], lb_ref[...])


def _pad_to(a, axis, mult):
    size = a.shape[axis]
    pad = (-size) % mult
    if pad == 0:
        return a
    widths = [(0, 0)] * a.ndim
    widths[axis] = (0, pad)
    return jnp.pad(a, widths)


def _mix_ffn(x2, ya, yb, yc, w_out, mix_w, mix_b, wg, wu, wd, ln_w, ln_b, li, layer, alpha, tm):
    t, d = x2.shape
    row = lambda w: pl.BlockSpec((tm, w), lambda i: (i, 0))
    return pl.pallas_call(
        functools.partial(_mix_ffn_kernel, alpha=alpha),
        grid=(t // tm,),
        in_specs=[row(d), row(ya.shape[1]), row(yb.shape[1]), row(yc.shape[1]),
                  _layer_of(w_out.shape, layer, True), _layer_of(mix_w.shape, layer), _layer_of(mix_b.shape, layer),
                  _layer_of(wg.shape, li, True), _layer_of(wu.shape, li, True), _layer_of(wd.shape, li, True),
                  _layer_of(ln_w.shape, layer), _layer_of(ln_b.shape, layer)],
        out_specs=row(d),
        out_shape=jax.ShapeDtypeStruct((t, d), F32),
        scratch_shapes=[pltpu.VMEM((tm, d), F32), pltpu.VMEM((tm, d), F32)],
        compiler_params=_cparams(("parallel",)),
        name="mix_ffn_ln",
    )(x2, ya, yb, yc, w_out, mix_w, mix_b, wg, wu, wd, ln_w, ln_b)


def _moe_plan(sel, n_experts, tile):
    e_flat = sel.reshape(-1)
    onehot = (e_flat[None, :] == jnp.arange(n_experts)[:, None]).astype(jnp.int32)
    csum = jnp.cumsum(onehot, axis=1)
    counts = csum[:, -1]
    rank = jnp.sum(onehot * csum, axis=0) - 1
    padded = (counts + tile - 1) // tile * tile
    ends = jnp.cumsum(padded)
    starts = ends - padded
    pos = jnp.sum(onehot * starts[:, None], axis=0) + rank
    n_tiles = e_flat.shape[0] // tile + n_experts
    first_row = jnp.arange(n_tiles, dtype=jnp.int32)[:, None] * tile
    tile_expert = jnp.minimum(jnp.sum((first_row >= ends[None, :]).astype(jnp.int32), axis=1), n_experts - 1)
    return (pos.astype(jnp.int32), tile_expert.astype(jnp.int32), (ends[-1:] // tile).astype(jnp.int32),
            (starts + counts).astype(jnp.int32), ends.astype(jnp.int32))


def _dispatch_kernel(lo_ref, hi_ref, nu_ref, pos_ref, x_ref, xs_ref, zero_ref, sem, *, n_experts):
    tm = x_ref.shape[0]

    def row_copy(r, dst):
        return pltpu.make_async_copy(x_ref.at[pl.ds(r, 1)], xs_ref.at[pl.ds(dst, 1)], sem)

    def start(r, c):
        row_copy(r, pos_ref[2 * r]).start(priority=0)
        row_copy(r, pos_ref[2 * r + 1]).start(priority=1)
        return c

    lax.fori_loop(0, tm, start, 0, unroll=DMA_UNROLL)
    for _ in range(TOP_K):
        pltpu.make_async_copy(x_ref, xs_ref.at[pl.ds(0, tm)], sem).wait()

    @pl.when(pl.program_id(0) == 0)
    def _():
        zero_ref[...] = jnp.zeros_like(zero_ref)

        def zero_row(dst):
            return pltpu.make_async_copy(zero_ref.at[pl.ds(0, 1)], xs_ref.at[pl.ds(dst, 1)], sem)

        def zero_tile(i):
            return pltpu.make_async_copy(zero_ref, xs_ref.at[pl.ds(pl.multiple_of(i * tm, tm), tm)], sem)

        for e in range(n_experts):
            lax.fori_loop(lo_ref[e], hi_ref[e], lambda r, c: (zero_row(r).start(), c)[1], 0)
            lax.fori_loop(lo_ref[e], hi_ref[e], lambda r, c: (zero_row(0).wait(), c)[1], 0)
        n_tiles = xs_ref.shape[0] // tm
        lax.fori_loop(nu_ref[0], n_tiles, lambda i, c: (zero_tile(i).start(), c)[1], 0)
        lax.fori_loop(nu_ref[0], n_tiles, lambda i, c: (zero_tile(0).wait(), c)[1], 0)


def _dispatch(x2, pos, pad_lo, pad_hi, n_used, n_rows, tm):
    t, d = x2.shape
    n_experts = pad_lo.shape[0]
    grid_spec = pltpu.PrefetchScalarGridSpec(
        num_scalar_prefetch=3,
        grid=(t // tm,),
        in_specs=[pl.BlockSpec((TOP_K * tm,), lambda i, *_: (i,), memory_space=pltpu.SMEM),
                  pl.BlockSpec((tm, d), lambda i, *_: (i, 0))],
        out_specs=pl.BlockSpec(memory_space=pl.ANY),
        scratch_shapes=[pltpu.VMEM((tm, d), F32), pltpu.SemaphoreType.DMA(())],
    )
    return pl.pallas_call(
        functools.partial(_dispatch_kernel, n_experts=n_experts),
        grid_spec=grid_spec,
        out_shape=jax.ShapeDtypeStruct((n_rows, d), F32),
        compiler_params=_cparams(("arbitrary",)),
        name="moe_dispatch",
    )(pad_lo, pad_hi, n_used, pos, x2)


def _expert_kernel(te_ref, nu_ref, x_ref, wg_ref, wu_ref, wd_ref, y_ref, acc_ref):
    used = pl.program_id(0) < nu_ref[0]

    @pl.when(used)
    def _():
        acc_ref[...] = jnp.zeros_like(acc_ref)
        _swiglu_acc(x_ref[...].astype(BF16), wg_ref, wu_ref, wd_ref, acc_ref, MXU_DIM)
        y_ref[...] = acc_ref[...]

    @pl.when(jnp.logical_not(used))
    def _():
        y_ref[...] = jnp.zeros_like(y_ref)


def _experts(xs, tile_expert, n_used, wg, wu, wd, li, tile):
    n_rows, d = xs.shape
    ffp = wg.shape[3]
    row = pl.BlockSpec((tile, d), lambda i, te, nu: (i, 0))
    grid_spec = pltpu.PrefetchScalarGridSpec(
        num_scalar_prefetch=2,
        grid=(n_rows // tile,),
        in_specs=[row,
                  pl.BlockSpec((None, None, d, ffp), lambda i, te, nu: (li, te[i], 0, 0)),
                  pl.BlockSpec((None, None, d, ffp), lambda i, te, nu: (li, te[i], 0, 0)),
                  pl.BlockSpec((None, None, ffp, d), lambda i, te, nu: (li, te[i], 0, 0))],
        out_specs=row,
        scratch_shapes=[pltpu.VMEM((tile, d), F32)],
    )
    return pl.pallas_call(
        _expert_kernel,
        grid_spec=grid_spec,
        out_shape=jax.ShapeDtypeStruct((n_rows, d), F32),
        compiler_params=_cparams(("arbitrary",)),
        name="moe_experts",
    )(tile_expert, n_used, xs, wg, wu, wd)


def _combine_kernel(pos_ref, x_ref, g_ref, ys_ref, lw_ref, lb_ref, o_ref, y0_ref, y1_ref, sem, *, alpha, n_experts):
    tm = x_ref.shape[0]

    def row_copy(src, buf, r):
        return pltpu.make_async_copy(ys_ref.at[pl.ds(src, 1)], buf.at[pl.ds(r, 1)], sem)

    def start(r, c):
        row_copy(pos_ref[2 * r], y0_ref, r).start(priority=0)
        row_copy(pos_ref[2 * r + 1], y1_ref, r).start(priority=1)
        return c

    lax.fori_loop(0, tm, start, 0, unroll=DMA_UNROLL)
    for buf in (y0_ref, y1_ref):
        pltpu.make_async_copy(ys_ref.at[pl.ds(0, tm)], buf, sem).wait()
    g = g_ref[...]
    lane = lax.broadcasted_iota(jnp.int32, g.shape, 1)
    w0 = jnp.sum(jnp.where(lane == n_experts + 2, g, 0.0), axis=1, keepdims=True)
    w1 = jnp.sum(jnp.where(lane == n_experts + 3, g, 0.0), axis=1, keepdims=True)
    f = w0 * y0_ref[...] + w1 * y1_ref[...]
    o_ref[...] = _layer_norm_rows(alpha * x_ref[...] + f, lw_ref[...], lb_ref[...])


def _combine(x2, gates, pos, ys, ln_w, ln_b, layer, n_experts, alpha, tm):
    t, d = x2.shape
    row = pl.BlockSpec((tm, d), lambda i: (i, 0))
    return pl.pallas_call(
        functools.partial(_combine_kernel, alpha=alpha, n_experts=n_experts),
        grid=(t // tm,),
        in_specs=[pl.BlockSpec((TOP_K * tm,), lambda i: (i,), memory_space=pltpu.SMEM), row,
                  pl.BlockSpec((tm, LANES), lambda i: (i, 0)), pl.BlockSpec(memory_space=pl.ANY),
                  _layer_of(ln_w.shape, layer), _layer_of(ln_b.shape, layer)],
        out_specs=row,
        out_shape=jax.ShapeDtypeStruct((t, d), F32),
        scratch_shapes=[pltpu.VMEM((tm, d), F32), pltpu.VMEM((tm, d), F32), pltpu.SemaphoreType.DMA(())],
        compiler_params=_cparams(("arbitrary",)),
        name="moe_combine_ln",
    )(pos, x2, gates, ys, ln_w, ln_b)


def _moe(x2, gates, wg, wu, wd, ln_w, ln_b, li, layer, alpha, tm):
    t, d = x2.shape
    n_experts = wg.shape[1]
    sel = gates[:, n_experts:n_experts + TOP_K].astype(jnp.int32)
    pos, tile_expert, n_used, pad_lo, pad_hi = _moe_plan(sel, n_experts, tm)
    xs = _dispatch(x2, pos, pad_lo, pad_hi, n_used, tile_expert.shape[0] * tm, tm)
    ys = _experts(xs, tile_expert, n_used, wg, wu, wd, li, tm)
    return _combine(x2, gates, pos, ys, ln_w, ln_b, layer, n_experts, alpha, tm)


def kernel(x, w_in, ssm_lam_re, ssm_lam_im, ssm_log_dt, ssm_b_re, ssm_b_im, ssm_c_re, ssm_c_im, ssm_d, ssm_glu_w, ssm_glu_b, ssm_norm_w, ret_gn_w, ret_gn_b, sb_norm_w, w_out, ln_mix_w, ln_mix_b, ffn_w_gate, ffn_w_up, ffn_w_down, moe_router, moe_w_gate, moe_w_up, moe_w_down, ln_ffn_w, ln_ffn_b):
    batch, seq, d = x.shape
    depth = w_in.shape[0]
    alpha = (2.0 * depth) ** 0.25
    ssm_w = ssm_d.shape[1]
    ret_w = ret_gn_w.shape[1]
    sb_w = sb_norm_w.shape[1]
    t = batch * seq
    tm = min(ROW_TILE, t)
    vec = lambda a: a[:, None, :]

    w_in_bf, w_out_bf = w_in.astype(BF16), w_out.astype(BF16)
    glu_w_bf = ssm_glu_w.astype(BF16)
    ffn_g = _pad_to(ffn_w_gate, 2, MXU_DIM).astype(BF16)
    ffn_u = _pad_to(ffn_w_up, 2, MXU_DIM).astype(BF16)
    ffn_d = _pad_to(ffn_w_down, 1, MXU_DIM).astype(BF16)
    moe_g = _pad_to(moe_w_gate, 3, MXU_DIM).astype(BF16)
    moe_u = _pad_to(moe_w_up, 3, MXU_DIM).astype(BF16)
    moe_d = _pad_to(moe_w_down, 2, MXU_DIM).astype(BF16)
    n_experts = moe_router.shape[2]
    router_hl = jnp.concatenate(_split_bf16(_pad_to(moe_router, 2, LANES)), axis=2)
    table_in = _s5_table_inputs(ssm_lam_re, ssm_lam_im, ssm_log_dt, ssm_b_re, ssm_b_im, ssm_c_re, ssm_c_im, ssm_d)
    s5_mt, s5_wt, s5_vt = _s5_tables(*table_in[:7])
    lam_pos, lam_sup = table_in[7:]
    ret_tables = _retention_tables(seq, ret_w // RET_HEAD_DIM)
    s5_rows = min(S5_ROWS, seq // S5_CHUNK)

    h = x.reshape(t, d)
    for layer in range(depth):
        a, rq, rk, rv, rg, sq, sk, sv = _proj(h, w_in_bf, layer, ssm_w, ret_w, sb_w, tm)
        xp = _s5_states(a, s5_wt, lam_pos, lam_sup, layer, batch, s5_rows)
        y_ssm = _s5_outputs(a, s5_mt, xp, s5_vt, glu_w_bf, vec(ssm_glu_b), vec(ssm_norm_w), layer, s5_rows)
        y_ret = _retention_mixer(rq, rk, rv, rg, ret_tables, vec(ret_gn_w), vec(ret_gn_b), layer, batch, seq)
        y_sb = _sb_mixer(sq, sk, sv, vec(sb_norm_w), layer, batch, seq)
        li = layer // 2
        if layer % 2 == 0:
            h = _mix_ffn(h, y_ssm, y_ret, y_sb, w_out_bf, vec(ln_mix_w), vec(ln_mix_b), ffn_g, ffn_u, ffn_d,
                         vec(ln_ffn_w), vec(ln_ffn_b), li, layer, alpha, min(FFN_ROW_TILE, t))
        else:
            h, gates = _mix(h, y_ssm, y_ret, y_sb, w_out_bf, vec(ln_mix_w), vec(ln_mix_b), layer, alpha, tm,
                            router=(router_hl, li, n_experts))
            h = _moe(h, gates, moe_g, moe_u, moe_d, vec(ln_ffn_w), vec(ln_ffn_b), li, layer, alpha, tm)
    return h.reshape(batch, seq, d)
```
